```python
import jax, jax.numpy as jnp
from jax import lax
import numpy as np

D_MODEL = 2048
BATCH = 8
SEQ = 2048
DEPTH = 1

CHUNK = 64
Q_BLOCK = 128
MLA_HEADS = 16
QK_NOPE = 128
QK_ROPE = 64
V_HEAD = D_MODEL // MLA_HEADS
Q_LORA = 512
KV_LORA = 256
ROPE_THETA = 10000.0
CONV_CH = 1024
CONV_WIDTH = 31
MEM_LEN = 256
MEM_HEADS = 4
MEM_HEAD_DIM = D_MODEL // MEM_HEADS
D_FF = ((-(-8 * D_MODEL // 3)) + 255) // 256 * 256
N_IN = Q_LORA + KV_LORA + QK_ROPE + 2 * CONV_CH + 2 * D_MODEL
EPS = 1e-6
NEG_INF = -1e30

kernel_name = "hybrid_mla_conformer_gated_block"


def rms_norm(x, g):
    xf = x.astype(jnp.float32)
    y = xf * lax.rsqrt(jnp.mean(xf * xf, axis=-1, keepdims=True) + EPS)
    return (y * g.astype(jnp.float32)).astype(x.dtype)


def layer_norm(x, g, b):
    xf = x.astype(jnp.float32)
    mu = jnp.mean(xf, axis=-1, keepdims=True)
    var = jnp.mean(jnp.square(xf - mu), axis=-1, keepdims=True)
    y = (xf - mu) * lax.rsqrt(var + EPS)
    return (y * g.astype(jnp.float32) + b.astype(jnp.float32)).astype(x.dtype)


def rope_tables(positions, dtype):
    inv_freq = 1.0 / (ROPE_THETA ** (jnp.arange(0, QK_ROPE, 2, dtype=jnp.float32) / QK_ROPE))
    ang = positions.astype(jnp.float32)[..., None] * inv_freq
    return jnp.cos(ang)[:, :, None, :].astype(dtype), jnp.sin(ang)[:, :, None, :].astype(dtype)


def apply_rope(t, cos, sin):
    t1, t2 = jnp.split(t, 2, axis=-1)
    return jnp.concatenate([t1 * cos - t2 * sin, t1 * sin + t2 * cos], axis=-1)


def chunk_causal_attention(q, k, v, scale):
    S = q.shape[1]
    chunk_id = jnp.arange(S) // CHUNK
    outs = []
    for qb in range(S // Q_BLOCK):
        q0 = qb * Q_BLOCK
        kend = q0 + Q_BLOCK
        s = jnp.einsum('bqhd,bkhd->bhqk', q[:, q0:kend], k[:, :kend]).astype(jnp.float32) * scale
        mask = chunk_id[q0:kend, None] >= chunk_id[None, :kend]
        p = jax.nn.softmax(jnp.where(mask, s, NEG_INF), axis=-1).astype(v.dtype)
        outs.append(jnp.einsum('bhqk,bkhd->bqhd', p, v[:, :kend]))
    return jnp.concatenate(outs, axis=1)


def mla_branch(c_q, c_kv, k_rope, cos, sin, norm_cq, w_uq, norm_ckv, w_ukv):
    B, S, _ = c_q.shape
    q = (rms_norm(c_q, norm_cq) @ w_uq).reshape(B, S, MLA_HEADS, QK_NOPE + QK_ROPE)
    q = jnp.concatenate([q[..., :QK_NOPE], apply_rope(q[..., QK_NOPE:], cos, sin)], axis=-1)
    kv = (rms_norm(c_kv, norm_ckv) @ w_ukv).reshape(B, S, MLA_HEADS, QK_NOPE + V_HEAD)
    k_nope, v = kv[..., :QK_NOPE], kv[..., QK_NOPE:]
    k_r = apply_rope(k_rope[:, :, None, :], cos, sin)
    k = jnp.concatenate([k_nope, jnp.broadcast_to(k_r, (B, S, MLA_HEADS, QK_ROPE))], axis=-1)
    o = chunk_causal_attention(q, k, v, (QK_NOPE + QK_ROPE) ** -0.5)
    return o.reshape(B, S, MLA_HEADS * V_HEAD)


def conformer_conv_branch(u_in, b_conv_in, w_dw, b_dw, ln_g, ln_b, w_pw2, b_pw2):
    a, g = jnp.split(u_in + b_conv_in, 2, axis=-1)
    u = a * jax.nn.sigmoid(g)
    u = lax.conv_general_dilated(
        u, w_dw[:, None, :].astype(u.dtype), window_strides=(1,),
        padding=[(CONV_WIDTH - 1, 0)], dimension_numbers=('NWC', 'WIO', 'NWC'),
        feature_group_count=CONV_CH) + b_dw
    u = jax.nn.silu(layer_norm(u, ln_g, ln_b))
    return u @ w_pw2 + b_pw2


def memory_cross_attention(h, mem_n, w_q_mem, w_kv_mem, w_o_mem):
    B, S, _ = h.shape
    q = (h @ w_q_mem).reshape(B, S, MEM_HEADS, MEM_HEAD_DIM)
    kv = (mem_n @ w_kv_mem).reshape(B, mem_n.shape[1], 2, MEM_HEADS, MEM_HEAD_DIM)
    k, v = kv[:, :, 0], kv[:, :, 1]
    s = jnp.einsum('bqhd,bmhd->bhqm', q, k).astype(jnp.float32) * MEM_HEAD_DIM ** -0.5
    p = jax.nn.softmax(s, axis=-1).astype(v.dtype)
    o = jnp.einsum('bhqm,bmhd->bqhd', p, v).reshape(B, S, D_MODEL)
    return o @ w_o_mem


def swiglu(h, w_gate, w_up, w_down):
    return (jax.nn.silu(h @ w_gate) * (h @ w_up)) @ w_down


def setup_inputs(seed: int = 0) -> dict:
    key = jax.random.key(seed)
    ks = jax.random.split(key, 32)
    f32 = jnp.float32
    L = DEPTH

    def w(k, shape, fan_in):
        return jax.random.normal(k, shape, f32) * (fan_in ** -0.5)

    def gain(k, shape):
        return 1.0 + 0.02 * jax.random.normal(k, shape, f32)

    def bias(k, shape):
        return 0.02 * jax.random.normal(k, shape, f32)

    x = jax.random.normal(ks[0], (BATCH, SEQ, D_MODEL), f32)
    mem = jax.random.normal(ks[1], (BATCH, MEM_LEN, D_MODEL), f32)
    offset = jax.random.randint(ks[2], (BATCH, 1), 0, 4096, dtype=jnp.int32)
    positions = (offset + jnp.arange(SEQ, dtype=jnp.int32)[None, :]).astype(jnp.int32)
    return {
        "x": x,
        "mem": mem,
        "positions": positions,
        "norm_mix": gain(ks[3], (L, D_MODEL)),
        "w_in": w(ks[4], (L, D_MODEL, N_IN), D_MODEL),
        "b_conv_in": bias(ks[5], (L, 2 * CONV_CH)),
        "norm_cq": gain(ks[6], (L, Q_LORA)),
        "w_uq": w(ks[7], (L, Q_LORA, MLA_HEADS * (QK_NOPE + QK_ROPE)), Q_LORA),
        "norm_ckv": gain(ks[8], (L, KV_LORA)),
        "w_ukv": w(ks[9], (L, KV_LORA, MLA_HEADS * (QK_NOPE + V_HEAD)), KV_LORA),
        "w_dw": w(ks[10], (L, CONV_WIDTH, CONV_CH), CONV_WIDTH),
        "b_dw": bias(ks[11], (L, CONV_CH)),
        "ln_conv_g": gain(ks[12], (L, CONV_CH)),
        "ln_conv_b": bias(ks[13], (L, CONV_CH)),
        "w_pw2": w(ks[14], (L, CONV_CH, D_MODEL), CONV_CH),
        "b_pw2": bias(ks[15], (L, D_MODEL)),
        "w_o": w(ks[16], (L, D_MODEL, D_MODEL), D_MODEL),
        "norm_cross": gain(ks[17], (L, D_MODEL)),
        "norm_mem": gain(ks[18], (L, D_MODEL)),
        "w_q_mem": w(ks[19], (L, D_MODEL, D_MODEL), D_MODEL),
        "w_kv_mem": w(ks[20], (L, D_MODEL, 2 * D_MODEL), D_MODEL),
        "w_o_mem": w(ks[21], (L, D_MODEL, D_MODEL), D_MODEL),
        "norm_ffn": gain(ks[22], (L, D_MODEL)),
        "w_gate": w(ks[23], (L, D_MODEL, D_FF), D_MODEL),
        "w_up": w(ks[24], (L, D_MODEL, D_FF), D_MODEL),
        "w_down": w(ks[25], (L, D_FF, D_MODEL), D_FF),
        "norm_final": gain(ks[26], (D_MODEL,)),
    }


def reference(x, mem, positions, norm_mix, w_in, b_conv_in, norm_cq, w_uq, norm_ckv, w_ukv,
              w_dw, b_dw, ln_conv_g, ln_conv_b, w_pw2, b_pw2, w_o, norm_cross, norm_mem,
              w_q_mem, w_kv_mem, w_o_mem, norm_ffn, w_gate, w_up, w_down, norm_final):
    cos, sin = rope_tables(positions, x.dtype)
    split_at = [Q_LORA, Q_LORA + KV_LORA, Q_LORA + KV_LORA + QK_ROPE,
                Q_LORA + KV_LORA + QK_ROPE + 2 * CONV_CH,
                Q_LORA + KV_LORA + QK_ROPE + 2 * CONV_CH + D_MODEL]
    for l in range(DEPTH):
        h = rms_norm(x, norm_mix[l])
        c_q, c_kv, k_rope, u_in, gate_a, gate_b = jnp.split(h @ w_in[l], split_at, axis=-1)
        y_a = mla_branch(c_q, c_kv, k_rope, cos, sin, norm_cq[l], w_uq[l], norm_ckv[l], w_ukv[l])
        y_b = conformer_conv_branch(u_in, b_conv_in[l], w_dw[l], b_dw[l], ln_conv_g[l],
                                    ln_conv_b[l], w_pw2[l], b_pw2[l])
        merged = jax.nn.sigmoid(gate_a) * y_a + jax.nn.sigmoid(gate_b) * y_b
        x = x + merged @ w_o[l]
        mem_n = rms_norm(mem, norm_mem[l])
        x = x + memory_cross_attention(rms_norm(x, norm_cross[l]), mem_n,
                                       w_q_mem[l], w_kv_mem[l], w_o_mem[l])
        x = x + swiglu(rms_norm(x, norm_ffn[l]), w_gate[l], w_up[l], w_down[l])
    return rms_norm(x, norm_final)
```

```python
import functools

import jax
import jax.numpy as jnp
from jax import lax
from jax.experimental import pallas as pl
from jax.experimental.pallas import tpu as pltpu

F32 = jnp.float32
BF16 = jnp.bfloat16

D_MODEL = 2048
CHUNK = 64
MLA_HEADS = 16
QK_NOPE = 128
QK_ROPE = 64
V_HEAD = 128
Q_LORA = 512
KV_LORA = 256
ROPE_THETA = 10000.0
CONV_CH = 1024
CONV_WIDTH = 31
MEM_HEADS = 4
MEM_HEAD_DIM = D_MODEL // MEM_HEADS
EPS = 1e-6
NEG_INF = -1e30

LANES = 128
HEAD_PAD = 256
LAT_W = Q_LORA + KV_LORA + 2 * QK_ROPE
HALO = 32
VMEM_LIMIT = 56 * 1024 * 1024


def _cparams(n_axes):
    return pltpu.CompilerParams(
        dimension_semantics=("arbitrary",) * n_axes,
        vmem_limit_bytes=VMEM_LIMIT)


def _sigmoid(x):
    return 1.0 / (1.0 + jnp.exp(-x))


def _rms(x, g):
    ms = jnp.mean(x * x, axis=-1, keepdims=True)
    return x * lax.rsqrt(ms + EPS) * g


def _resident(shape):
    return pl.BlockSpec(shape, lambda i: (0,) * len(shape))


def _rms_kernel(x_ref, g_ref, o_ref):
    o_ref[...] = _rms(x_ref[...], g_ref[...]).astype(o_ref.dtype)


def rmsnorm_bf16(x, g, tm):
    m, d = x.shape
    return pl.pallas_call(
        _rms_kernel,
        grid=(m // tm,),
        in_specs=[pl.BlockSpec((tm, d), lambda i: (i, 0)), _resident((1, d))],
        out_specs=pl.BlockSpec((tm, d), lambda i: (i, 0)),
        out_shape=jax.ShapeDtypeStruct((m, d), BF16),
        compiler_params=_cparams(1),
        name="rmsnorm",
    )(x, g.reshape(1, d))


def _lat_kernel(h_ref, w_ref, o_ref):
    o_ref[...] = jnp.dot(h_ref[...], w_ref[...], preferred_element_type=F32)


def proj_latent(h, w, tm):
    m, k = h.shape
    n = w.shape[1]
    return pl.pallas_call(
        _lat_kernel,
        grid=(m // tm,),
        in_specs=[pl.BlockSpec((tm, k), lambda i: (i, 0)), _resident((k, n))],
        out_specs=pl.BlockSpec((tm, n), lambda i: (i, 0)),
        out_shape=jax.ShapeDtypeStruct((m, n), F32),
        compiler_params=_cparams(1),
        name="proj_latent",
    )(h, w)


def _glu_kernel(h_ref, w_ref, b_ref, o_ref):
    acc = jnp.dot(h_ref[...], w_ref[...], preferred_element_type=F32) + b_ref[...]
    o_ref[...] = acc[:, :CONV_CH] * _sigmoid(acc[:, CONV_CH:])


def proj_glu(h, w, b, tm):
    m, k = h.shape
    n = w.shape[1]
    return pl.pallas_call(
        _glu_kernel,
        grid=(m // tm,),
        in_specs=[pl.BlockSpec((tm, k), lambda i: (i, 0)), _resident((k, n)),
                  _resident((1, n))],
        out_specs=pl.BlockSpec((tm, n // 2), lambda i: (i, 0)),
        out_shape=jax.ShapeDtypeStruct((m, n // 2), F32),
        compiler_params=_cparams(1),
        name="proj_glu",
    )(h, w, b.reshape(1, n))


def _gates_kernel(h_ref, w_ref, o_ref):
    acc = jnp.dot(h_ref[...], w_ref[...], preferred_element_type=F32)
    o_ref[...] = _sigmoid(acc).astype(o_ref.dtype)


def proj_gates(h, w, tm, tn):
    m, k = h.shape
    n = w.shape[1]
    return pl.pallas_call(
        _gates_kernel,
        grid=(n // tn, m // tm),
        in_specs=[pl.BlockSpec((tm, k), lambda j, i: (i, 0)),
                  pl.BlockSpec((k, tn), lambda j, i: (0, j))],
        out_specs=pl.BlockSpec((tm, tn), lambda j, i: (i, j)),
        out_shape=jax.ShapeDtypeStruct((m, n), BF16),
        compiler_params=_cparams(2),
        name="proj_gates",
    )(h, w)


def _q_kernel(c_ref, g_ref, w_ref, tab_ref, o_ref):
    cn = _rms(c_ref[...], g_ref[...]).astype(BF16)
    tab = tab_ref[...]
    for hh in range(MLA_HEADS):
        sl = slice(hh * HEAD_PAD, (hh + 1) * HEAD_PAD)
        acc = jnp.dot(cn, w_ref[:, sl], preferred_element_type=F32)
        o_ref[:, sl] = (acc * tab).astype(o_ref.dtype)


def proj_q(lat, g, w, tab, tm):
    m = lat.shape[0]
    n = w.shape[1]
    return pl.pallas_call(
        _q_kernel,
        grid=(m // tm,),
        in_specs=[pl.BlockSpec((tm, Q_LORA), lambda i: (i, 0)),
                  _resident((1, Q_LORA)),
                  _resident((Q_LORA, n)),
                  pl.BlockSpec((tm, HEAD_PAD), lambda i: (i, 0))],
        out_specs=pl.BlockSpec((tm, n), lambda i: (i, 0)),
        out_shape=jax.ShapeDtypeStruct((m, n), BF16),
        compiler_params=_cparams(1),
        name="proj_q",
    )(lat, g.reshape(1, Q_LORA), w, tab)


def _kv_kernel(c_ref, g_ref, kl_ref, tab_ref, w_ref, k_ref, v_ref):
    cn = _rms(c_ref[...], g_ref[...]).astype(BF16)
    prod = kl_ref[...] * tab_ref[...]
    krot2 = (prod + pltpu.roll(prod, QK_ROPE, 1)).astype(k_ref.dtype)
    nk = MLA_HEADS * QK_NOPE
    for pair in range(MLA_HEADS // 2):
        acc = jnp.dot(cn, w_ref[:, pair * 256:(pair + 1) * 256],
                      preferred_element_type=F32)
        for sub in range(2):
            hh = 2 * pair + sub
            k_ref[:, hh * HEAD_PAD:hh * HEAD_PAD + QK_NOPE] = (
                acc[:, sub * QK_NOPE:(sub + 1) * QK_NOPE].astype(k_ref.dtype))
            k_ref[:, hh * HEAD_PAD + QK_NOPE:(hh + 1) * HEAD_PAD] = krot2
    v_ref[...] = jnp.dot(cn, w_ref[:, nk:], preferred_element_type=F32).astype(v_ref.dtype)


def proj_kv(lat, g, w, tab, tm):
    m = lat.shape[0]
    ckv_blk = Q_LORA // KV_LORA
    kl_blk = (Q_LORA + KV_LORA) // LANES
    return pl.pallas_call(
        _kv_kernel,
        grid=(m // tm,),
        in_specs=[pl.BlockSpec((tm, KV_LORA), lambda i: (i, ckv_blk)),
                  _resident((1, KV_LORA)),
                  pl.BlockSpec((tm, LANES), lambda i: (i, kl_blk)),
                  pl.BlockSpec((tm, LANES), lambda i: (i, 0)),
                  _resident((KV_LORA, w.shape[1]))],
        out_specs=[pl.BlockSpec((tm, MLA_HEADS * HEAD_PAD), lambda i: (i, 0)),
                   pl.BlockSpec((tm, MLA_HEADS * V_HEAD), lambda i: (i, 0))],
        out_shape=[jax.ShapeDtypeStruct((m, MLA_HEADS * HEAD_PAD), BF16),
                   jax.ShapeDtypeStruct((m, MLA_HEADS * V_HEAD), BF16)],
        compiler_params=_cparams(1),
        name="proj_kv",
    )(lat, g.reshape(1, KV_LORA), lat, tab, w)


ATT_TQ = 256


def _attn_kernel(q_ref, k_ref, v_ref, o_ref, *, seq):
    tq = ATT_TQ
    row = lax.broadcasted_iota(jnp.int32, (tq, tq), 0) // CHUNK
    col = lax.broadcasted_iota(jnp.int32, (tq, tq), 1) // CHUNK
    diag_mask = col <= row
    dn = (((1,), (1,)), ((), ()))
    for qi in range(seq // tq):
        q0 = qi * tq
        q = q_ref[q0:q0 + tq, :]
        s_d = lax.dot_general(q, k_ref[q0:q0 + tq, :], dn, preferred_element_type=F32)
        s_d = jnp.where(diag_mask, s_d, NEG_INF)
        m = jnp.max(s_d, axis=-1, keepdims=True)
        if qi > 0:
            s_p = lax.dot_general(q, k_ref[0:q0, :], dn, preferred_element_type=F32)
            m = jnp.maximum(m, jnp.max(s_p, axis=-1, keepdims=True))
        p_d = jnp.exp(s_d - m)
        l = jnp.sum(p_d, axis=-1, keepdims=True)
        o = jnp.dot(p_d.astype(BF16), v_ref[q0:q0 + tq, :], preferred_element_type=F32)
        if qi > 0:
            p_p = jnp.exp(s_p - m)
            l = l + jnp.sum(p_p, axis=-1, keepdims=True)
            o = o + jnp.dot(p_p.astype(BF16), v_ref[0:q0, :], preferred_element_type=F32)
        o_ref[q0:q0 + tq, :] = (o * (1.0 / l)).astype(o_ref.dtype)


def attention(q, k, v, batch, seq):
    return pl.pallas_call(
        functools.partial(_attn_kernel, seq=seq),
        grid=(batch, MLA_HEADS),
        in_specs=[pl.BlockSpec((seq, HEAD_PAD), lambda b, h: (b, h)),
                  pl.BlockSpec((seq, HEAD_PAD), lambda b, h: (b, h)),
                  pl.BlockSpec((seq, V_HEAD), lambda b, h: (b, h))],
        out_specs=pl.BlockSpec((seq, V_HEAD), lambda b, h: (b, h)),
        out_shape=jax.ShapeDtypeStruct((batch * seq, MLA_HEADS * V_HEAD), BF16),
        compiler_params=_cparams(2),
        name="mla_attention",
    )(q, k, v)


CONV_ROWS = 16


def _conv_kernel(u_ref, w_ref, bdw_ref, g_ref, b_ref, o_ref, ext_ref, *, tm):
    s = pl.program_id(1)

    @pl.when(s == 0)
    def _():
        ext_ref[0:HALO, :] = jnp.zeros((HALO, CONV_CH), F32)

    @pl.when(s > 0)
    def _():
        ext_ref[0:HALO, :] = ext_ref[tm:tm + HALO, :]

    ext_ref[HALO:, :] = u_ref[...]
    base = HALO - (CONV_WIDTH - 1)
    for c in range(tm // CONV_ROWS):
        r0 = c * CONV_ROWS
        acc = jnp.broadcast_to(bdw_ref[...], (CONV_ROWS, CONV_CH))
        for j in range(CONV_WIDTH):
            acc = acc + w_ref[j:j + 1, :] * ext_ref[r0 + base + j:r0 + base + j + CONV_ROWS, :]
        mu = jnp.mean(acc, axis=-1, keepdims=True)
        d = acc - mu
        var = jnp.mean(d * d, axis=-1, keepdims=True)
        y = d * lax.rsqrt(var + EPS) * g_ref[...] + b_ref[...]
        o_ref[r0:r0 + CONV_ROWS, :] = (y * _sigmoid(y)).astype(o_ref.dtype)


def conv_branch(u, w_dw, b_dw, ln_g, ln_b, batch, seq, tm):
    w_pad = jnp.zeros((HALO, CONV_CH), F32).at[:CONV_WIDTH].set(w_dw)
    nt = seq // tm
    vec = lambda a: a.reshape(1, CONV_CH)
    const = lambda shape: pl.BlockSpec(shape, lambda b, s: (0, 0))
    return pl.pallas_call(
        functools.partial(_conv_kernel, tm=tm),
        grid=(batch, nt),
        in_specs=[pl.BlockSpec((tm, CONV_CH), lambda b, s: (b * nt + s, 0)),
                  const((HALO, CONV_CH)), const((1, CONV_CH)), const((1, CONV_CH)),
                  const((1, CONV_CH))],
        out_specs=pl.BlockSpec((tm, CONV_CH), lambda b, s: (b * nt + s, 0)),
        out_shape=jax.ShapeDtypeStruct((batch * seq, CONV_CH), BF16),
        scratch_shapes=[pltpu.VMEM((tm + HALO, CONV_CH), F32)],
        compiler_params=_cparams(2),
        name="conv_branch",
    )(u, w_pad, vec(b_dw), vec(ln_g), vec(ln_b))


def _mix_kernel(v_ref, ya_ref, sga_ref, sgb_ref, x_ref, wpw_ref, bpw_ref, wo_ref, g_ref,
                x1_ref, h1_ref):
    yb = jnp.dot(v_ref[...], wpw_ref[...], preferred_element_type=F32) + bpw_ref[...]
    merged = (sga_ref[...].astype(F32) * ya_ref[...].astype(F32)
              + sgb_ref[...].astype(F32) * yb)
    x1 = x_ref[...] + jnp.dot(merged.astype(BF16), wo_ref[...], preferred_element_type=F32)
    x1_ref[...] = x1
    h1_ref[...] = _rms(x1, g_ref[...]).astype(h1_ref.dtype)


def mix(v, ya, sg, x, w_pw2, b_pw2, w_o, g, tm):
    m, d = x.shape
    row = lambda w: pl.BlockSpec((tm, w), lambda i: (i, 0))
    return pl.pallas_call(
        _mix_kernel,
        grid=(m // tm,),
        in_specs=[row(CONV_CH), row(d),
                  pl.BlockSpec((tm, d), lambda i: (i, 0)),
                  pl.BlockSpec((tm, d), lambda i: (i, 1)),
                  row(d),
                  _resident((CONV_CH, d)), _resident((1, d)), _resident((d, d)),
                  _resident((1, d))],
        out_specs=[row(d), row(d)],
        out_shape=[jax.ShapeDtypeStruct((m, d), F32), jax.ShapeDtypeStruct((m, d), BF16)],
        compiler_params=_cparams(1),
        name="mix_out_proj",
    )(v, ya, sg, sg, x, w_pw2, b_pw2.reshape(1, d), w_o, g.reshape(1, d))


def _memkv_kernel(m_ref, w_ref, o_ref):
    o_ref[...] = jnp.dot(m_ref[...], w_ref[...], preferred_element_type=F32).astype(o_ref.dtype)


def mem_kv(mem_n, w, tn):
    m, k = mem_n.shape
    n = w.shape[1]
    return pl.pallas_call(
        _memkv_kernel,
        grid=(n // tn,),
        in_specs=[_resident((m, k)), pl.BlockSpec((k, tn), lambda j: (0, j))],
        out_specs=pl.BlockSpec((m, tn), lambda j: (0, j)),
        out_shape=jax.ShapeDtypeStruct((m, n), BF16),
        compiler_params=_cparams(1),
        name="mem_kv",
    )(mem_n, w)


def _cross_kernel(h_ref, x_ref, kv_ref, wq_ref, wo_ref, g_ref, x2_ref, h2_ref, o_scr):
    qm = jnp.dot(h_ref[...], wq_ref[...], preferred_element_type=F32)
    dn = (((1,), (1,)), ((), ()))
    scale = MEM_HEAD_DIM ** -0.5
    for hd in range(MEM_HEADS):
        sl = slice(hd * MEM_HEAD_DIM, (hd + 1) * MEM_HEAD_DIM)
        vsl = slice(D_MODEL + hd * MEM_HEAD_DIM, D_MODEL + (hd + 1) * MEM_HEAD_DIM)
        s = lax.dot_general(qm[:, sl].astype(BF16), kv_ref[:, sl], dn,
                            preferred_element_type=F32) * scale
        m = jnp.max(s, axis=-1, keepdims=True)
        p = jnp.exp(s - m)
        l = jnp.sum(p, axis=-1, keepdims=True)
        o = jnp.dot(p.astype(BF16), kv_ref[:, vsl], preferred_element_type=F32)
        o_scr[:, sl] = (o * (1.0 / l)).astype(o_scr.dtype)
    x2 = x_ref[...] + jnp.dot(o_scr[...], wo_ref[...], preferred_element_type=F32)
    x2_ref[...] = x2
    h2_ref[...] = _rms(x2, g_ref[...]).astype(h2_ref.dtype)


def cross_attention(h1, x1, kvm, w_q, w_o, g, batch, seq, mem_len, tm):
    d = D_MODEL
    nt = seq // tm
    row = pl.BlockSpec((tm, d), lambda b, s: (b * nt + s, 0))
    const = lambda shape: pl.BlockSpec(shape, lambda b, s: (0, 0))
    return pl.pallas_call(
        _cross_kernel,
        grid=(batch, nt),
        in_specs=[row, row,
                  pl.BlockSpec((mem_len, 2 * d), lambda b, s: (b, 0)),
                  const((d, d)), const((d, d)), const((1, d))],
        out_specs=[row, row],
        out_shape=[jax.ShapeDtypeStruct((batch * seq, d), F32),
                   jax.ShapeDtypeStruct((batch * seq, d), BF16)],
        scratch_shapes=[pltpu.VMEM((tm, d), BF16)],
        compiler_params=_cparams(2),
        name="cross_attention",
    )(h1, x1, kvm, w_q, w_o, g.reshape(1, d))


FFN_TN = 512


def _ffn_up_kernel(h_ref, w_ref, o_ref):
    acc = jnp.dot(h_ref[...], w_ref[...], preferred_element_type=F32)
    gt = acc[:, :FFN_TN]
    o_ref[...] = (gt * _sigmoid(gt) * acc[:, FFN_TN:]).astype(o_ref.dtype)


def ffn_up(h, w_gu, tm):
    m, k = h.shape
    nj = w_gu.shape[1] // (2 * FFN_TN)
    return pl.pallas_call(
        _ffn_up_kernel,
        grid=(nj, m // tm),
        in_specs=[pl.BlockSpec((tm, k), lambda j, i: (i, 0)),
                  pl.BlockSpec((k, 2 * FFN_TN), lambda j, i: (0, j))],
        out_specs=pl.BlockSpec((tm, FFN_TN), lambda j, i: (i, j)),
        out_shape=jax.ShapeDtypeStruct((m, nj * FFN_TN), BF16),
        compiler_params=_cparams(2),
        name="ffn_up",
    )(h, w_gu)


def _ffn_down_kernel(a_ref, w_ref, x_ref, g_ref, o_ref, acc_ref, *, nk):
    kk = pl.program_id(1)
    part = jnp.dot(a_ref[...], w_ref[...], preferred_element_type=F32)

    @pl.when(kk == 0)
    def _():
        acc_ref[...] = x_ref[...] + part

    @pl.when(kk > 0)
    def _():
        acc_ref[...] = acc_ref[...] + part

    @pl.when(kk == nk - 1)
    def _():
        o_ref[...] = _rms(acc_ref[...], g_ref[...])


def ffn_down(a, w, x, g, tm, tk):
    m, kdim = a.shape
    d = w.shape[1]
    nk = kdim // tk
    return pl.pallas_call(
        functools.partial(_ffn_down_kernel, nk=nk),
        grid=(m // tm, nk),
        in_specs=[pl.BlockSpec((tm, tk), lambda i, kk: (i, kk)),
                  pl.BlockSpec((tk, d), lambda i, kk: (kk, 0)),
                  pl.BlockSpec((tm, d), lambda i, kk: (i, 0)),
                  pl.BlockSpec((1, d), lambda i, kk: (0, 0))],
        out_specs=pl.BlockSpec((tm, d), lambda i, kk: (i, 0)),
        out_shape=jax.ShapeDtypeStruct((m, d), F32),
        scratch_shapes=[pltpu.VMEM((tm, d), F32)],
        compiler_params=_cparams(2),
        name="ffn_down",
    )(a, w, x, g.reshape(1, d))


def _swap_halves(w):
    half = w.shape[-1] // 2
    return jnp.concatenate([w[..., half:], w[..., :half]], axis=-1)


def _rope_tables(positions):
    inv_freq = 1.0 / (ROPE_THETA ** (jnp.arange(0, QK_ROPE, 2, dtype=F32) / QK_ROPE))
    ang = positions.astype(F32).reshape(-1, 1) * inv_freq
    cos, sin = jnp.cos(ang), jnp.sin(ang)
    cc = jnp.concatenate([cos, cos], axis=-1)
    ss = jnp.concatenate([-sin, sin], axis=-1)
    tab_k = jnp.concatenate([cc, ss], axis=-1)
    scale = (QK_NOPE + QK_ROPE) ** -0.5
    tab_q = scale * jnp.concatenate([jnp.ones((ang.shape[0], QK_NOPE), F32), tab_k], axis=-1)
    return tab_q, tab_k


def kernel(x, mem, positions, norm_mix, w_in, b_conv_in, norm_cq, w_uq, norm_ckv, w_ukv, w_dw, b_dw, ln_conv_g, ln_conv_b, w_pw2, b_pw2, w_o, norm_cross, norm_mem, w_q_mem, w_kv_mem, w_o_mem, norm_ffn, w_gate, w_up, w_down, norm_final):
    batch, seq, d = x.shape
    mem_len = mem.shape[1]
    t = batch * seq
    xf = x.reshape(t, d)
    tab_q, tab_k = _rope_tables(positions)
    l = 0

    wi = w_in[l]
    o_kr = Q_LORA + KV_LORA
    o_u = o_kr + QK_ROPE
    o_g = o_u + 2 * CONV_CH
    w_lat = jnp.concatenate(
        [wi[:, :o_u], _swap_halves(wi[:, o_kr:o_u])], axis=1).astype(BF16)
    w_u = wi[:, o_u:o_g].astype(BF16)
    w_g = wi[:, o_g:].astype(BF16)
    wq3 = w_uq[l].reshape(Q_LORA, MLA_HEADS, QK_NOPE + QK_ROPE)
    w_q = jnp.concatenate(
        [wq3, _swap_halves(wq3[..., QK_NOPE:])], axis=-1).reshape(Q_LORA, -1).astype(BF16)
    wkv3 = w_ukv[l].reshape(KV_LORA, MLA_HEADS, QK_NOPE + V_HEAD)
    w_kv = jnp.concatenate(
        [wkv3[..., :QK_NOPE].reshape(KV_LORA, -1), wkv3[..., QK_NOPE:].reshape(KV_LORA, -1)],
        axis=1).astype(BF16)
    d_ff = w_gate.shape[-1]
    w_gu = jnp.stack(
        [w_gate[l].reshape(d, d_ff // FFN_TN, FFN_TN), w_up[l].reshape(d, d_ff // FFN_TN, FFN_TN)],
        axis=2).reshape(d, 2 * d_ff).astype(BF16)

    h0 = rmsnorm_bf16(xf, norm_mix[l], tm=512)
    lat = proj_latent(h0, w_lat, tm=1024)
    u = proj_glu(h0, w_u, b_conv_in[l], tm=512)
    sg = proj_gates(h0, w_g, tm=1024, tn=1024)
    q = proj_q(lat, norm_cq[l], w_q, tab_q, tm=512)
    k, v = proj_kv(lat, norm_ckv[l], w_kv, tab_k, tm=512)
    y_a = attention(q, k, v, batch, seq)
    cv = conv_branch(u, w_dw[l], b_dw[l], ln_conv_g[l], ln_conv_b[l], batch, seq, tm=256)
    x1, h1 = mix(cv, y_a, sg, xf, w_pw2[l].astype(BF16), b_pw2[l], w_o[l].astype(BF16),
                 norm_cross[l], tm=256)

    mem_n = rmsnorm_bf16(mem.reshape(batch * mem_len, d), norm_mem[l], tm=512)
    kvm = mem_kv(mem_n, w_kv_mem[l].astype(BF16), tn=1024)
    x2, h2 = cross_attention(h1, x1, kvm, w_q_mem[l].astype(BF16), w_o_mem[l].astype(BF16),
                             norm_ffn[l], batch, seq, mem_len, tm=256)

    a = ffn_up(h2, w_gu, tm=1024)
    out = ffn_down(a, w_down[l].astype(BF16), x2, norm_final, tm=512, tk=d_ff // 4)
    return out.reshape(batch, seq, d)
```

```python
import functools

import jax
import jax.numpy as jnp
from jax import lax
from jax.experimental import pallas as pl
from jax.experimental.pallas import tpu as pltpu

F32 = jnp.float32
BF16 = jnp.bfloat16

D_MODEL = 2048
CHUNK = 64
MLA_HEADS = 16
QK_NOPE = 128
QK_ROPE = 64
V_HEAD = 128
Q_LORA = 512
KV_LORA = 256
ROPE_THETA = 10000.0
CONV_CH = 1024
CONV_WIDTH = 31
MEM_HEADS = 4
MEM_HEAD_DIM = D_MODEL // MEM_HEADS
EPS = 1e-6
NEG_INF = -1e30

LANES = 128
HEAD_PAD = 256
LAT_W = Q_LORA + KV_LORA + 2 * QK_ROPE
HALO = 32
VMEM_LIMIT = 56 * 1024 * 1024


def _cparams(n_axes):
    return pltpu.CompilerParams(
        dimension_semantics=("arbitrary",) * n_axes,
        vmem_limit_bytes=VMEM_LIMIT)


def _sigmoid(x):
    return 1.0 / (1.0 + jnp.exp(-x))


def _rms(x, g):
    ms = jnp.mean(x * x, axis=-1, keepdims=True)
    return x * lax.rsqrt(ms + EPS) * g


def _resident(shape):
    zeros = (0,) * len(shape)
    return pl.BlockSpec(shape, lambda *_: zeros, pipeline_mode=pl.Buffered(1))


def _rms_kernel(x_ref, g_ref, o_ref):
    o_ref[...] = _rms(x_ref[...], g_ref[...]).astype(o_ref.dtype)


def rmsnorm_bf16(x, g, tm):
    m, d = x.shape
    return pl.pallas_call(
        _rms_kernel,
        grid=(m // tm,),
        in_specs=[pl.BlockSpec((tm, d), lambda i: (i, 0)), _resident((1, d))],
        out_specs=pl.BlockSpec((tm, d), lambda i: (i, 0)),
        out_shape=jax.ShapeDtypeStruct((m, d), BF16),
        compiler_params=_cparams(1),
        name="rmsnorm",
    )(x, g.reshape(1, d))


def _lat_kernel(h_ref, w_ref, o_ref):
    o_ref[...] = jnp.dot(h_ref[...], w_ref[...], preferred_element_type=F32)


def proj_latent(h, w, tm):
    m, k = h.shape
    n = w.shape[1]
    return pl.pallas_call(
        _lat_kernel,
        grid=(m // tm,),
        in_specs=[pl.BlockSpec((tm, k), lambda i: (i, 0)), _resident((k, n))],
        out_specs=pl.BlockSpec((tm, n), lambda i: (i, 0)),
        out_shape=jax.ShapeDtypeStruct((m, n), F32),
        compiler_params=_cparams(1),
        name="proj_latent",
    )(h, w)


def _glu_kernel(h_ref, w_ref, b_ref, o_ref):
    acc = jnp.dot(h_ref[...], w_ref[...], preferred_element_type=F32) + b_ref[...]
    o_ref[...] = acc[:, :CONV_CH] * _sigmoid(acc[:, CONV_CH:])


def proj_glu(h, w, b, tm):
    m, k = h.shape
    n = w.shape[1]
    return pl.pallas_call(
        _glu_kernel,
        grid=(m // tm,),
        in_specs=[pl.BlockSpec((tm, k), lambda i: (i, 0)), _resident((k, n)),
                  _resident((1, n))],
        out_specs=pl.BlockSpec((tm, n // 2), lambda i: (i, 0)),
        out_shape=jax.ShapeDtypeStruct((m, n // 2), F32),
        compiler_params=_cparams(1),
        name="proj_glu",
    )(h, w, b.reshape(1, n))


def _gates_kernel(h_ref, w_ref, o_ref):
    acc = jnp.dot(h_ref[...], w_ref[...], preferred_element_type=F32)
    o_ref[...] = _sigmoid(acc).astype(o_ref.dtype)


def proj_gates(h, w, tm, tn):
    m, k = h.shape
    n = w.shape[1]
    return pl.pallas_call(
        _gates_kernel,
        grid=(n // tn, m // tm),
        in_specs=[pl.BlockSpec((tm, k), lambda j, i: (i, 0)),
                  pl.BlockSpec((k, tn), lambda j, i: (0, j))],
        out_specs=pl.BlockSpec((tm, tn), lambda j, i: (i, j)),
        out_shape=jax.ShapeDtypeStruct((m, n), BF16),
        compiler_params=_cparams(2),
        name="proj_gates",
    )(h, w)


def _q_kernel(c_ref, g_ref, w_ref, tab_ref, o_ref):
    cn = _rms(c_ref[...], g_ref[...]).astype(BF16)
    tab = tab_ref[...]
    for hh in range(MLA_HEADS):
        sl = slice(hh * HEAD_PAD, (hh + 1) * HEAD_PAD)
        acc = jnp.dot(cn, w_ref[:, sl], preferred_element_type=F32)
        o_ref[:, sl] = (acc * tab).astype(o_ref.dtype)


def proj_q(lat, g, w, tab, tm):
    m = lat.shape[0]
    n = w.shape[1]
    return pl.pallas_call(
        _q_kernel,
        grid=(m // tm,),
        in_specs=[pl.BlockSpec((tm, Q_LORA), lambda i: (i, 0)),
                  _resident((1, Q_LORA)),
                  _resident((Q_LORA, n)),
                  pl.BlockSpec((tm, HEAD_PAD), lambda i: (i, 0))],
        out_specs=pl.BlockSpec((tm, n), lambda i: (i, 0)),
        out_shape=jax.ShapeDtypeStruct((m, n), BF16),
        compiler_params=_cparams(1),
        name="proj_q",
    )(lat, g.reshape(1, Q_LORA), w, tab)


def _kv_kernel(c_ref, g_ref, kl_ref, tab_ref, w_ref, k_ref, v_ref):
    cn = _rms(c_ref[...], g_ref[...]).astype(BF16)
    prod = kl_ref[...] * tab_ref[...]
    krot2 = (prod + pltpu.roll(prod, QK_ROPE, 1)).astype(k_ref.dtype)
    nk = MLA_HEADS * QK_NOPE
    for pair in range(MLA_HEADS // 2):
        acc = jnp.dot(cn, w_ref[:, pair * 256:(pair + 1) * 256],
                      preferred_element_type=F32)
        for sub in range(2):
            hh = 2 * pair + sub
            k_ref[:, hh * HEAD_PAD:hh * HEAD_PAD + QK_NOPE] = (
                acc[:, sub * QK_NOPE:(sub + 1) * QK_NOPE].astype(k_ref.dtype))
            k_ref[:, hh * HEAD_PAD + QK_NOPE:(hh + 1) * HEAD_PAD] = krot2
    v_ref[...] = jnp.dot(cn, w_ref[:, nk:], preferred_element_type=F32).astype(v_ref.dtype)


def proj_kv(lat, g, w, tab, tm):
    m = lat.shape[0]
    ckv_blk = Q_LORA // KV_LORA
    kl_blk = (Q_LORA + KV_LORA) // LANES
    return pl.pallas_call(
        _kv_kernel,
        grid=(m // tm,),
        in_specs=[pl.BlockSpec((tm, KV_LORA), lambda i: (i, ckv_blk)),
                  _resident((1, KV_LORA)),
                  pl.BlockSpec((tm, LANES), lambda i: (i, kl_blk)),
                  pl.BlockSpec((tm, LANES), lambda i: (i, 0)),
                  _resident((KV_LORA, w.shape[1]))],
        out_specs=[pl.BlockSpec((tm, MLA_HEADS * HEAD_PAD), lambda i: (i, 0)),
                   pl.BlockSpec((tm, MLA_HEADS * V_HEAD), lambda i: (i, 0))],
        out_shape=[jax.ShapeDtypeStruct((m, MLA_HEADS * HEAD_PAD), BF16),
                   jax.ShapeDtypeStruct((m, MLA_HEADS * V_HEAD), BF16)],
        compiler_params=_cparams(1),
        name="proj_kv",
    )(lat, g.reshape(1, KV_LORA), lat, tab, w)


ATT_TQ = 256


def _attn_kernel(q_ref, k_ref, v_ref, o_ref, *, seq):
    tq = ATT_TQ
    row = lax.broadcasted_iota(jnp.int32, (tq, tq), 0) // CHUNK
    col = lax.broadcasted_iota(jnp.int32, (tq, tq), 1) // CHUNK
    diag_mask = col <= row
    dn = (((1,), (1,)), ((), ()))
    for qi in range(seq // tq):
        q0 = qi * tq
        q = q_ref[q0:q0 + tq, :]
        s_d = lax.dot_general(q, k_ref[q0:q0 + tq, :], dn, preferred_element_type=F32)
        s_d = jnp.where(diag_mask, s_d, NEG_INF)
        m = jnp.max(s_d, axis=-1, keepdims=True)
        if qi > 0:
            s_p = lax.dot_general(q, k_ref[0:q0, :], dn, preferred_element_type=F32)
            m = jnp.maximum(m, jnp.max(s_p, axis=-1, keepdims=True))
        p_d = jnp.exp(s_d - m)
        l = jnp.sum(p_d, axis=-1, keepdims=True)
        o = jnp.dot(p_d.astype(BF16), v_ref[q0:q0 + tq, :], preferred_element_type=F32)
        if qi > 0:
            p_p = jnp.exp(s_p - m)
            l = l + jnp.sum(p_p, axis=-1, keepdims=True)
            o = o + jnp.dot(p_p.astype(BF16), v_ref[0:q0, :], preferred_element_type=F32)
        o_ref[q0:q0 + tq, :] = (o * (1.0 / l)).astype(o_ref.dtype)


def attention(q, k, v, batch, seq):
    return pl.pallas_call(
        functools.partial(_attn_kernel, seq=seq),
        grid=(batch, MLA_HEADS),
        in_specs=[pl.BlockSpec((seq, HEAD_PAD), lambda b, h: (b, h)),
                  pl.BlockSpec((seq, HEAD_PAD), lambda b, h: (b, h)),
                  pl.BlockSpec((seq, V_HEAD), lambda b, h: (b, h))],
        out_specs=pl.BlockSpec((seq, V_HEAD), lambda b, h: (b, h)),
        out_shape=jax.ShapeDtypeStruct((batch * seq, MLA_HEADS * V_HEAD), BF16),
        compiler_params=_cparams(2),
        name="mla_attention",
    )(q, k, v)


SUBLANES = 8
CONV_GROUPS = 4


def _conv_kernel(u_ref, w_ref, bdw_ref, g_ref, b_ref, o_ref, ext_ref, sh_ref, *, tm):
    s = pl.program_id(1)
    n_sh = tm + HALO - SUBLANES

    @pl.when(s == 0)
    def _():
        ext_ref[0:HALO, :] = jnp.zeros((HALO, CONV_CH), F32)

    @pl.when(s > 0)
    def _():
        ext_ref[0:HALO, :] = ext_ref[tm:tm + HALO, :]

    ext_ref[HALO:, :] = u_ref[...]
    for r in range(1, SUBLANES):
        sh_ref[r - 1, :, :] = ext_ref[r:r + n_sh, :]
    base = HALO - (CONV_WIDTH - 1)
    taps = sorted(range(CONV_WIDTH), key=lambda j: ((base + j) % SUBLANES, j))
    for c in range(tm // (SUBLANES * CONV_GROUPS)):
        r0 = c * SUBLANES * CONV_GROUPS
        accs = [bdw_ref[...]] * CONV_GROUPS
        for j in taps:
            phase = (base + j) % SUBLANES
            wj = w_ref[j]
            for gi in range(CONV_GROUPS):
                a0 = r0 + gi * SUBLANES + base + j - phase
                if phase == 0:
                    src = ext_ref[a0:a0 + SUBLANES, :]
                else:
                    src = sh_ref[phase - 1, a0:a0 + SUBLANES, :]
                accs[gi] = accs[gi] + wj * src
        for gi in range(CONV_GROUPS):
            acc = accs[gi]
            mu = jnp.mean(acc, axis=-1, keepdims=True)
            d = acc - mu
            var = jnp.mean(d * d, axis=-1, keepdims=True)
            y = d * lax.rsqrt(var + EPS) * g_ref[...] + b_ref[...]
            rows = slice(r0 + gi * SUBLANES, r0 + (gi + 1) * SUBLANES)
            o_ref[rows, :] = (y * _sigmoid(y)).astype(o_ref.dtype)


def conv_branch(u, w_dw, b_dw, ln_g, ln_b, batch, seq, tm):
    w8 = jnp.broadcast_to(w_dw[:, None, :], (CONV_WIDTH, SUBLANES, CONV_CH))
    nt = seq // tm
    vec = lambda a: jnp.broadcast_to(a.reshape(1, CONV_CH), (SUBLANES, CONV_CH))
    return pl.pallas_call(
        functools.partial(_conv_kernel, tm=tm),
        grid=(batch, nt),
        in_specs=[pl.BlockSpec((tm, CONV_CH), lambda b, s: (b * nt + s, 0)),
                  _resident((CONV_WIDTH, SUBLANES, CONV_CH)), _resident((SUBLANES, CONV_CH)),
                  _resident((SUBLANES, CONV_CH)), _resident((SUBLANES, CONV_CH))],
        out_specs=pl.BlockSpec((tm, CONV_CH), lambda b, s: (b * nt + s, 0)),
        out_shape=jax.ShapeDtypeStruct((batch * seq, CONV_CH), BF16),
        scratch_shapes=[pltpu.VMEM((tm + HALO, CONV_CH), F32),
                        pltpu.VMEM((SUBLANES - 1, tm + HALO - SUBLANES, CONV_CH), F32)],
        compiler_params=_cparams(2),
        name="conv_branch",
    )(u, w8, vec(b_dw), vec(ln_g), vec(ln_b))


def _mix_kernel(v_ref, ya_ref, sga_ref, sgb_ref, x_ref, wpw_ref, bpw_ref, wo_ref, g_ref,
                x1_ref, h1_ref):
    yb = jnp.dot(v_ref[...], wpw_ref[...], preferred_element_type=F32) + bpw_ref[...]
    merged = (sga_ref[...].astype(F32) * ya_ref[...].astype(F32)
              + sgb_ref[...].astype(F32) * yb)
    x1 = x_ref[...] + jnp.dot(merged.astype(BF16), wo_ref[...], preferred_element_type=F32)
    x1_ref[...] = x1
    h1_ref[...] = _rms(x1, g_ref[...]).astype(h1_ref.dtype)


def mix(v, ya, sg, x, w_pw2, b_pw2, w_o, g, tm):
    m, d = x.shape
    row = lambda w: pl.BlockSpec((tm, w), lambda i: (i, 0))
    return pl.pallas_call(
        _mix_kernel,
        grid=(m // tm,),
        in_specs=[row(CONV_CH), row(d),
                  pl.BlockSpec((tm, d), lambda i: (i, 0)),
                  pl.BlockSpec((tm, d), lambda i: (i, 1)),
                  row(d),
                  _resident((CONV_CH, d)), _resident((1, d)), _resident((d, d)),
                  _resident((1, d))],
        out_specs=[row(d), row(d)],
        out_shape=[jax.ShapeDtypeStruct((m, d), F32), jax.ShapeDtypeStruct((m, d), BF16)],
        compiler_params=_cparams(1),
        name="mix_out_proj",
    )(v, ya, sg, sg, x, w_pw2, b_pw2.reshape(1, d), w_o, g.reshape(1, d))


def _memkv_kernel(m_ref, w_ref, o_ref):
    o_ref[...] = jnp.dot(m_ref[...], w_ref[...], preferred_element_type=F32).astype(o_ref.dtype)


def mem_kv(mem_n, w, tn):
    m, k = mem_n.shape
    n = w.shape[1]
    return pl.pallas_call(
        _memkv_kernel,
        grid=(n // tn,),
        in_specs=[_resident((m, k)), pl.BlockSpec((k, tn), lambda j: (0, j))],
        out_specs=pl.BlockSpec((m, tn), lambda j: (0, j)),
        out_shape=jax.ShapeDtypeStruct((m, n), BF16),
        compiler_params=_cparams(1),
        name="mem_kv",
    )(mem_n, w)


def _cross_kernel(h_ref, x_ref, kv_ref, wq_ref, wo_ref, g_ref, x2_ref, h2_ref, o_scr):
    qm = jnp.dot(h_ref[...], wq_ref[...], preferred_element_type=F32)
    dn = (((1,), (1,)), ((), ()))
    scale = MEM_HEAD_DIM ** -0.5
    for hd in range(MEM_HEADS):
        sl = slice(hd * MEM_HEAD_DIM, (hd + 1) * MEM_HEAD_DIM)
        vsl = slice(D_MODEL + hd * MEM_HEAD_DIM, D_MODEL + (hd + 1) * MEM_HEAD_DIM)
        s = lax.dot_general(qm[:, sl].astype(BF16), kv_ref[:, sl], dn,
                            preferred_element_type=F32) * scale
        m = jnp.max(s, axis=-1, keepdims=True)
        p = jnp.exp(s - m)
        l = jnp.sum(p, axis=-1, keepdims=True)
        o = jnp.dot(p.astype(BF16), kv_ref[:, vsl], preferred_element_type=F32)
        o_scr[:, sl] = (o * (1.0 / l)).astype(o_scr.dtype)
    x2 = x_ref[...] + jnp.dot(o_scr[...], wo_ref[...], preferred_element_type=F32)
    x2_ref[...] = x2
    h2_ref[...] = _rms(x2, g_ref[...]).astype(h2_ref.dtype)


def cross_attention(h1, x1, kvm, w_q, w_o, g, batch, seq, mem_len, tm):
    d = D_MODEL
    nt = seq // tm
    row = pl.BlockSpec((tm, d), lambda b, s: (b * nt + s, 0))
    return pl.pallas_call(
        _cross_kernel,
        grid=(batch, nt),
        in_specs=[row, row,
                  pl.BlockSpec((mem_len, 2 * d), lambda b, s: (b, 0)),
                  _resident((d, d)), _resident((d, d)), _resident((1, d))],
        out_specs=[row, row],
        out_shape=[jax.ShapeDtypeStruct((batch * seq, d), F32),
                   jax.ShapeDtypeStruct((batch * seq, d), BF16)],
        scratch_shapes=[pltpu.VMEM((tm, d), BF16)],
        compiler_params=_cparams(2),
        name="cross_attention",
    )(h1, x1, kvm, w_q, w_o, g.reshape(1, d))


FFN_TN = 512


def _ffn_up_kernel(h_ref, wg_ref, wu_ref, o_ref):
    h = h_ref[...]
    gt = jnp.dot(h, wg_ref[...], preferred_element_type=F32)
    up = jnp.dot(h, wu_ref[...], preferred_element_type=F32)
    o_ref[...] = (gt * _sigmoid(gt) * up).astype(o_ref.dtype)


def ffn_up(h, w_gate, w_up, tm):
    m, k = h.shape
    n = w_gate.shape[1]
    wspec = pl.BlockSpec((k, FFN_TN), lambda j, i: (0, j))
    return pl.pallas_call(
        _ffn_up_kernel,
        grid=(n // FFN_TN, m // tm),
        in_specs=[pl.BlockSpec((tm, k), lambda j, i: (i, 0)), wspec, wspec],
        out_specs=pl.BlockSpec((tm, FFN_TN), lambda j, i: (i, j)),
        out_shape=jax.ShapeDtypeStruct((m, n), BF16),
        compiler_params=_cparams(2),
        name="ffn_up",
    )(h, w_gate, w_up)


def _ffn_down_kernel(a_ref, w_ref, x_ref, g_ref, o_ref):
    x3 = x_ref[...] + jnp.dot(a_ref[...], w_ref[...], preferred_element_type=F32)
    o_ref[...] = _rms(x3, g_ref[...])


def ffn_down(a, w, x, g, tm):
    m, kdim = a.shape
    d = w.shape[1]
    return pl.pallas_call(
        _ffn_down_kernel,
        grid=(m // tm,),
        in_specs=[pl.BlockSpec((tm, kdim), lambda i: (i, 0)),
                  _resident((kdim, d)),
                  pl.BlockSpec((tm, d), lambda i: (i, 0)),
                  _resident((1, d))],
        out_specs=pl.BlockSpec((tm, d), lambda i: (i, 0)),
        out_shape=jax.ShapeDtypeStruct((m, d), F32),
        compiler_params=_cparams(1),
        name="ffn_down",
    )(a, w, x, g.reshape(1, d))


def _swap_halves(w):
    half = w.shape[-1] // 2
    return jnp.concatenate([w[..., half:], w[..., :half]], axis=-1)


def _rope_tables(positions):
    inv_freq = 1.0 / (ROPE_THETA ** (jnp.arange(0, QK_ROPE, 2, dtype=F32) / QK_ROPE))
    ang = positions.astype(F32).reshape(-1, 1) * inv_freq
    cos, sin = jnp.cos(ang), jnp.sin(ang)
    cc = jnp.concatenate([cos, cos], axis=-1)
    ss = jnp.concatenate([-sin, sin], axis=-1)
    tab_k = jnp.concatenate([cc, ss], axis=-1)
    scale = (QK_NOPE + QK_ROPE) ** -0.5
    tab_q = scale * jnp.concatenate([jnp.ones((ang.shape[0], QK_NOPE), F32), tab_k], axis=-1)
    return tab_q, tab_k


def kernel(x, mem, positions, norm_mix, w_in, b_conv_in, norm_cq, w_uq, norm_ckv, w_ukv, w_dw, b_dw, ln_conv_g, ln_conv_b, w_pw2, b_pw2, w_o, norm_cross, norm_mem, w_q_mem, w_kv_mem, w_o_mem, norm_ffn, w_gate, w_up, w_down, norm_final):
    batch, seq, d = x.shape
    mem_len = mem.shape[1]
    t = batch * seq
    xf = x.reshape(t, d)
    tab_q, tab_k = _rope_tables(positions)
    l = 0

    wi = w_in[l]
    o_kr = Q_LORA + KV_LORA
    o_u = o_kr + QK_ROPE
    o_g = o_u + 2 * CONV_CH
    w_lat = jnp.concatenate(
        [wi[:, :o_u], _swap_halves(wi[:, o_kr:o_u])], axis=1).astype(BF16)
    w_u = wi[:, o_u:o_g].astype(BF16)
    w_g = wi[:, o_g:].astype(BF16)
    wq3 = w_uq[l].reshape(Q_LORA, MLA_HEADS, QK_NOPE + QK_ROPE)
    w_q = jnp.concatenate(
        [wq3, _swap_halves(wq3[..., QK_NOPE:])], axis=-1).reshape(Q_LORA, -1).astype(BF16)
    wkv3 = w_ukv[l].reshape(KV_LORA, MLA_HEADS, QK_NOPE + V_HEAD)
    w_kv = jnp.concatenate(
        [wkv3[..., :QK_NOPE].reshape(KV_LORA, -1), wkv3[..., QK_NOPE:].reshape(KV_LORA, -1)],
        axis=1).astype(BF16)

    h0 = rmsnorm_bf16(xf, norm_mix[l], tm=512)
    lat = proj_latent(h0, w_lat, tm=1024)
    u = proj_glu(h0, w_u, b_conv_in[l], tm=512)
    sg = proj_gates(h0, w_g, tm=1024, tn=1024)
    q = proj_q(lat, norm_cq[l], w_q, tab_q, tm=512)
    k, v = proj_kv(lat, norm_ckv[l], w_kv, tab_k, tm=512)
    y_a = attention(q, k, v, batch, seq)
    cv = conv_branch(u, w_dw[l], b_dw[l], ln_conv_g[l], ln_conv_b[l], batch, seq, tm=256)
    x1, h1 = mix(cv, y_a, sg, xf, w_pw2[l].astype(BF16), b_pw2[l], w_o[l].astype(BF16),
                 norm_cross[l], tm=256)

    mem_n = rmsnorm_bf16(mem.reshape(batch * mem_len, d), norm_mem[l], tm=512)
    kvm = mem_kv(mem_n, w_kv_mem[l].astype(BF16), tn=1024)
    x2, h2 = cross_attention(h1, x1, kvm, w_q_mem[l].astype(BF16), w_o_mem[l].astype(BF16),
                             norm_ffn[l], batch, seq, mem_len, tm=256)

    a = ffn_up(h2, w_gate[l].astype(BF16), w_up[l].astype(BF16), tm=1024)
    out = ffn_down(a, w_down[l].astype(BF16), x2, norm_final, tm=256)
    return out.reshape(batch, seq, d)
```

```python
import functools
import math

import jax
import jax.numpy as jnp
from jax import lax
from jax.experimental import pallas as pl
from jax.experimental.pallas import tpu as pltpu

F32 = jnp.float32
BF16 = jnp.bfloat16

D_MODEL = 2048
CHUNK = 64
MLA_HEADS = 16
QK_NOPE = 128
QK_ROPE = 64
V_HEAD = 128
Q_LORA = 512
KV_LORA = 256
ROPE_THETA = 10000.0
CONV_CH = 1024
CONV_WIDTH = 31
MEM_HEADS = 4
MEM_HEAD_DIM = D_MODEL // MEM_HEADS
EPS = 1e-6
NEG_INF = -1e30

LANES = 128
SUBLANES = 8
HEAD_PAD = 256
BF16_ROWS = 16
VT_ROWS = V_HEAD + BF16_ROWS
LAT_W = Q_LORA + KV_LORA + 2 * QK_ROPE
HALO = 32
VMEM_LIMIT = 56 * 1024 * 1024

_NT = (((1,), (1,)), ((), ()))


def _cparams(n_axes):
    return pltpu.CompilerParams(
        dimension_semantics=("arbitrary",) * n_axes,
        vmem_limit_bytes=VMEM_LIMIT)


def _sigmoid(x):
    return 1.0 / (1.0 + jnp.exp(-x))


def _rms(x, g):
    ms = jnp.mean(x * x, axis=-1, keepdims=True)
    return x * lax.rsqrt(ms + EPS) * g


def _resident(shape):
    zeros = (0,) * len(shape)
    return pl.BlockSpec(shape, lambda *_: zeros, pipeline_mode=pl.Buffered(1))


GATE_CHUNK = 1024


def _inproj_kernel(x_ref, g_ref, wl_ref, wu_ref, bu_ref, wg_ref, lat_ref, u_ref, sg_ref):
    h = _rms(x_ref[...], g_ref[...]).astype(BF16)
    lat_ref[...] = jnp.dot(h, wl_ref[...], preferred_element_type=F32)
    acc = jnp.dot(h, wu_ref[...], preferred_element_type=F32) + bu_ref[...]
    u_ref[...] = acc[:, :CONV_CH] * _sigmoid(acc[:, CONV_CH:])
    for c in range(sg_ref.shape[1] // GATE_CHUNK):
        sl = slice(c * GATE_CHUNK, (c + 1) * GATE_CHUNK)
        acc = jnp.dot(h, wg_ref[:, sl], preferred_element_type=F32)
        sg_ref[:, sl] = _sigmoid(acc).astype(sg_ref.dtype)


def in_proj(x, g, w_lat, w_u, b_u, w_g, tm):
    m, d = x.shape
    row = lambda w: pl.BlockSpec((tm, w), lambda i: (i, 0))
    n_lat, n_u, n_g = w_lat.shape[1], w_u.shape[1], w_g.shape[1]
    return pl.pallas_call(
        _inproj_kernel,
        grid=(m // tm,),
        in_specs=[row(d), _resident((1, d)), _resident((d, n_lat)), _resident((d, n_u)),
                  _resident((1, n_u)), _resident((d, n_g))],
        out_specs=[row(n_lat), row(n_u // 2), row(n_g)],
        out_shape=[jax.ShapeDtypeStruct((m, n_lat), F32),
                   jax.ShapeDtypeStruct((m, n_u // 2), F32),
                   jax.ShapeDtypeStruct((m, n_g), BF16)],
        compiler_params=_cparams(1),
        name="in_proj",
    )(x, g.reshape(1, d), w_lat, w_u, b_u.reshape(1, n_u), w_g)


def _q_kernel(c_ref, g_ref, wt_ref, tab_ref, o_ref):
    cn = _rms(c_ref[...], g_ref[...]).astype(BF16)
    tab = tab_ref[...]
    for hh in range(MLA_HEADS):
        rows = slice(hh * HEAD_PAD, (hh + 1) * HEAD_PAD)
        acc = lax.dot_general(wt_ref[rows, :], cn, _NT, preferred_element_type=F32)
        o_ref[rows, :] = (acc * tab).astype(o_ref.dtype)


def proj_q(lat, g, w_t, tab_t, tm):
    m = lat.shape[0]
    n = w_t.shape[0]
    return pl.pallas_call(
        _q_kernel,
        grid=(m // tm,),
        in_specs=[pl.BlockSpec((tm, Q_LORA), lambda i: (i, 0)),
                  _resident((1, Q_LORA)),
                  _resident((n, Q_LORA)),
                  pl.BlockSpec((HEAD_PAD, tm), lambda i: (0, i))],
        out_specs=pl.BlockSpec((n, tm), lambda i: (0, i)),
        out_shape=jax.ShapeDtypeStruct((n, m), BF16),
        compiler_params=_cparams(1),
        name="proj_q",
    )(lat, g.reshape(1, Q_LORA), w_t, tab_t)


def _kv_kernel(c_ref, g_ref, kl_ref, tab_ref, wk_ref, wvt_ref, k_ref, vt_ref):
    cn = _rms(c_ref[...], g_ref[...]).astype(BF16)
    prod = kl_ref[...] * tab_ref[...]
    krot2 = (prod + pltpu.roll(prod, QK_ROPE, 1)).astype(k_ref.dtype)
    for pair in range(MLA_HEADS // 2):
        acc = jnp.dot(cn, wk_ref[:, pair * 256:(pair + 1) * 256],
                      preferred_element_type=F32)
        for sub in range(2):
            hh = 2 * pair + sub
            k_ref[:, hh * HEAD_PAD:hh * HEAD_PAD + QK_NOPE] = (
                acc[:, sub * QK_NOPE:(sub + 1) * QK_NOPE].astype(k_ref.dtype))
            k_ref[:, hh * HEAD_PAD + QK_NOPE:(hh + 1) * HEAD_PAD] = krot2
    vt = lax.dot_general(wvt_ref[...], cn, _NT, preferred_element_type=F32).astype(vt_ref.dtype)
    ones = jnp.ones((BF16_ROWS, vt.shape[1]), vt_ref.dtype)
    for hh in range(MLA_HEADS):
        vt_ref[hh * VT_ROWS:hh * VT_ROWS + V_HEAD, :] = vt[hh * V_HEAD:(hh + 1) * V_HEAD, :]
        vt_ref[hh * VT_ROWS + V_HEAD:(hh + 1) * VT_ROWS, :] = ones


def proj_kv(lat, g, w_k, w_vt, tab, tm):
    m = lat.shape[0]
    ckv_blk = Q_LORA // KV_LORA
    kl_blk = (Q_LORA + KV_LORA) // LANES
    nk, nv = MLA_HEADS * HEAD_PAD, MLA_HEADS * VT_ROWS
    return pl.pallas_call(
        _kv_kernel,
        grid=(m // tm,),
        in_specs=[pl.BlockSpec((tm, KV_LORA), lambda i: (i, ckv_blk)),
                  _resident((1, KV_LORA)),
                  pl.BlockSpec((tm, LANES), lambda i: (i, kl_blk)),
                  pl.BlockSpec((tm, LANES), lambda i: (i, 0)),
                  _resident(w_k.shape), _resident(w_vt.shape)],
        out_specs=[pl.BlockSpec((tm, nk), lambda i: (i, 0)),
                   pl.BlockSpec((nv, tm), lambda i: (0, i))],
        out_shape=[jax.ShapeDtypeStruct((m, nk), BF16),
                   jax.ShapeDtypeStruct((nv, m), BF16)],
        compiler_params=_cparams(1),
        name="proj_kv",
    )(lat, g.reshape(1, KV_LORA), lat, tab, w_k, w_vt)


ATT_TQ = 512
ATT_HEADS = 2


def _attn_kernel(qt_ref, k_ref, vt_ref, o_ref, *, seq):
    tq = ATT_TQ
    g_diag = tq // SUBLANES
    key_chunk = lax.broadcasted_iota(jnp.int32, (g_diag, SUBLANES, tq), 0) // (CHUNK // SUBLANES)
    qry_chunk = lax.broadcasted_iota(jnp.int32, (g_diag, SUBLANES, tq), 2) // CHUNK
    diag_mask = key_chunk <= qry_chunk

    def scores(hh, qi):
        kend = (qi + 1) * tq
        s = jnp.dot(k_ref[0:kend, hh * HEAD_PAD:(hh + 1) * HEAD_PAD],
                    qt_ref[hh * HEAD_PAD:(hh + 1) * HEAD_PAD, qi * tq:kend],
                    preferred_element_type=F32)
        return s.reshape(kend // SUBLANES, SUBLANES, tq)

    stages = [(hh, qi) for qi in range(seq // tq) for hh in range(ATT_HEADS)]
    nxt = scores(*stages[0])
    for idx, (hh, qi) in enumerate(stages):
        q0, kend = qi * tq, (qi + 1) * tq
        s = nxt
        if idx + 1 < len(stages):
            nxt = scores(*stages[idx + 1])
        g_past = q0 // SUBLANES
        s_d = jnp.where(diag_mask, s[g_past:], NEG_INF)
        m8 = jnp.max(s_d, axis=0)
        if qi > 0:
            m8 = jnp.maximum(m8, jnp.max(s[:g_past], axis=0))
        m8 = jnp.broadcast_to(jnp.max(m8, axis=0, keepdims=True), (SUBLANES, tq))
        p = jnp.exp2(s_d - m8[None])
        if qi > 0:
            p = jnp.concatenate([jnp.exp2(s[:g_past] - m8[None]), p], axis=0)
        p = p.reshape(kend, tq).astype(BF16)
        ot = jnp.dot(vt_ref[hh * VT_ROWS:(hh + 1) * VT_ROWS, 0:kend], p,
                     preferred_element_type=F32)
        inv = 1.0 / ot[V_HEAD:V_HEAD + 1, :]
        o_ref[q0:kend, hh * V_HEAD:(hh + 1) * V_HEAD] = (
            (ot[:V_HEAD, :] * inv).T.astype(o_ref.dtype))


def attention(qt, k, vt, batch, seq):
    nh = ATT_HEADS
    return pl.pallas_call(
        functools.partial(_attn_kernel, seq=seq),
        grid=(batch, MLA_HEADS // nh),
        in_specs=[pl.BlockSpec((nh * HEAD_PAD, seq), lambda b, h: (h, b)),
                  pl.BlockSpec((seq, nh * HEAD_PAD), lambda b, h: (b, h)),
                  pl.BlockSpec((nh * VT_ROWS, seq), lambda b, h: (h, b))],
        out_specs=pl.BlockSpec((seq, nh * V_HEAD), lambda b, h: (b, h)),
        out_shape=jax.ShapeDtypeStruct((batch * seq, MLA_HEADS * V_HEAD), BF16),
        compiler_params=_cparams(2),
        name="mla_attention",
    )(qt, k, vt)


CONV_GROUPS = 4


def _conv_kernel(u_ref, w_ref, bdw_ref, g_ref, b_ref, o_ref, ext_ref, sh_ref, *, tm):
    s = pl.program_id(1)
    n_sh = tm + HALO - SUBLANES

    @pl.when(s == 0)
    def _():
        ext_ref[0:HALO, :] = jnp.zeros((HALO, CONV_CH), F32)

    @pl.when(s > 0)
    def _():
        ext_ref[0:HALO, :] = ext_ref[tm:tm + HALO, :]

    ext_ref[HALO:, :] = u_ref[...]
    for r in range(1, SUBLANES):
        sh_ref[r - 1, :, :] = ext_ref[r:r + n_sh, :]
    base = HALO - (CONV_WIDTH - 1)
    taps = sorted(range(CONV_WIDTH), key=lambda j: ((base + j) % SUBLANES, j))
    for c in range(tm // (SUBLANES * CONV_GROUPS)):
        r0 = c * SUBLANES * CONV_GROUPS
        accs = [bdw_ref[...]] * CONV_GROUPS
        for j in taps:
            phase = (base + j) % SUBLANES
            wj = w_ref[j]
            for gi in range(CONV_GROUPS):
                a0 = r0 + gi * SUBLANES + base + j - phase
                if phase == 0:
                    src = ext_ref[a0:a0 + SUBLANES, :]
                else:
                    src = sh_ref[phase - 1, a0:a0 + SUBLANES, :]
                accs[gi] = accs[gi] + wj * src
        for gi in range(CONV_GROUPS):
            acc = accs[gi]
            mu = jnp.mean(acc, axis=-1, keepdims=True)
            d = acc - mu
            var = jnp.mean(d * d, axis=-1, keepdims=True)
            y = d * lax.rsqrt(var + EPS) * g_ref[...] + b_ref[...]
            rows = slice(r0 + gi * SUBLANES, r0 + (gi + 1) * SUBLANES)
            o_ref[rows, :] = (y * _sigmoid(y)).astype(o_ref.dtype)


def conv_branch(u, w_dw, b_dw, ln_g, ln_b, batch, seq, tm):
    w8 = jnp.broadcast_to(w_dw[:, None, :], (CONV_WIDTH, SUBLANES, CONV_CH))
    nt = seq // tm
    vec = lambda a: jnp.broadcast_to(a.reshape(1, CONV_CH), (SUBLANES, CONV_CH))
    return pl.pallas_call(
        functools.partial(_conv_kernel, tm=tm),
        grid=(batch, nt),
        in_specs=[pl.BlockSpec((tm, CONV_CH), lambda b, s: (b * nt + s, 0)),
                  _resident((CONV_WIDTH, SUBLANES, CONV_CH)), _resident((SUBLANES, CONV_CH)),
                  _resident((SUBLANES, CONV_CH)), _resident((SUBLANES, CONV_CH))],
        out_specs=pl.BlockSpec((tm, CONV_CH), lambda b, s: (b * nt + s, 0)),
        out_shape=jax.ShapeDtypeStruct((batch * seq, CONV_CH), BF16),
        scratch_shapes=[pltpu.VMEM((tm + HALO, CONV_CH), F32),
                        pltpu.VMEM((SUBLANES - 1, tm + HALO - SUBLANES, CONV_CH), F32)],
        compiler_params=_cparams(2),
        name="conv_branch",
    )(u, w8, vec(b_dw), vec(ln_g), vec(ln_b))


def _mix_kernel(v_ref, ya_ref, sga_ref, sgb_ref, x_ref, wpw_ref, bpw_ref, wo_ref, g_ref,
                x1_ref, h1_ref):
    yb = jnp.dot(v_ref[...], wpw_ref[...], preferred_element_type=F32) + bpw_ref[...]
    merged = (sga_ref[...].astype(F32) * ya_ref[...].astype(F32)
              + sgb_ref[...].astype(F32) * yb)
    x1 = x_ref[...] + jnp.dot(merged.astype(BF16), wo_ref[...], preferred_element_type=F32)
    x1_ref[...] = x1
    h1_ref[...] = _rms(x1, g_ref[...]).astype(h1_ref.dtype)


def mix(v, ya, sg, x, w_pw2, b_pw2, w_o, g, tm):
    m, d = x.shape
    row = lambda w: pl.BlockSpec((tm, w), lambda i: (i, 0))
    return pl.pallas_call(
        _mix_kernel,
        grid=(m // tm,),
        in_specs=[row(CONV_CH), row(d),
                  pl.BlockSpec((tm, d), lambda i: (i, 0)),
                  pl.BlockSpec((tm, d), lambda i: (i, 1)),
                  row(d),
                  _resident((CONV_CH, d)), _resident((1, d)), _resident((d, d)),
                  _resident((1, d))],
        out_specs=[row(d), row(d)],
        out_shape=[jax.ShapeDtypeStruct((m, d), F32), jax.ShapeDtypeStruct((m, d), BF16)],
        compiler_params=_cparams(1),
        name="mix_out_proj",
    )(v, ya, sg, sg, x, w_pw2, b_pw2.reshape(1, d), w_o, g.reshape(1, d))


def _memkv_kernel(m_ref, g_ref, w_ref, o_ref, mn_ref):
    @pl.when(pl.program_id(0) == 0)
    def _():
        mn_ref[...] = _rms(m_ref[...], g_ref[...]).astype(mn_ref.dtype)

    o_ref[...] = jnp.dot(mn_ref[...], w_ref[...], preferred_element_type=F32).astype(o_ref.dtype)


def mem_kv(mem, g, w, tn):
    m, k = mem.shape
    n = w.shape[1]
    return pl.pallas_call(
        _memkv_kernel,
        grid=(n // tn,),
        in_specs=[_resident((m, k)), _resident((1, k)),
                  pl.BlockSpec((k, tn), lambda j: (0, j))],
        out_specs=pl.BlockSpec((m, tn), lambda j: (0, j)),
        out_shape=jax.ShapeDtypeStruct((m, n), BF16),
        scratch_shapes=[pltpu.VMEM((m, k), BF16)],
        compiler_params=_cparams(1),
        name="mem_kv",
    )(mem, g.reshape(1, k), w)


def _cross_kernel(h_ref, x_ref, kv_ref, wq_ref, wo_ref, g_ref, x2_ref, h2_ref, o_scr):
    qm = jnp.dot(h_ref[...], wq_ref[...], preferred_element_type=F32)
    scale = MEM_HEAD_DIM ** -0.5
    for hd in range(MEM_HEADS):
        sl = slice(hd * MEM_HEAD_DIM, (hd + 1) * MEM_HEAD_DIM)
        vsl = slice(D_MODEL + hd * MEM_HEAD_DIM, D_MODEL + (hd + 1) * MEM_HEAD_DIM)
        s = lax.dot_general(qm[:, sl].astype(BF16), kv_ref[:, sl], _NT,
                            preferred_element_type=F32) * scale
        m = jnp.max(s, axis=-1, keepdims=True)
        p = jnp.exp(s - m)
        l = jnp.sum(p, axis=-1, keepdims=True)
        o = jnp.dot(p.astype(BF16), kv_ref[:, vsl], preferred_element_type=F32)
        o_scr[:, sl] = (o * (1.0 / l)).astype(o_scr.dtype)
    x2 = x_ref[...] + jnp.dot(o_scr[...], wo_ref[...], preferred_element_type=F32)
    x2_ref[...] = x2
    h2_ref[...] = _rms(x2, g_ref[...]).astype(h2_ref.dtype)


def cross_attention(h1, x1, kvm, w_q, w_o, g, batch, seq, mem_len, tm):
    d = D_MODEL
    nt = seq // tm
    row = pl.BlockSpec((tm, d), lambda b, s: (b * nt + s, 0))
    return pl.pallas_call(
        _cross_kernel,
        grid=(batch, nt),
        in_specs=[row, row,
                  pl.BlockSpec((mem_len, 2 * d), lambda b, s: (b, 0)),
                  _resident((d, d)), _resident((d, d)), _resident((1, d))],
        out_specs=[row, row],
        out_shape=[jax.ShapeDtypeStruct((batch * seq, d), F32),
                   jax.ShapeDtypeStruct((batch * seq, d), BF16)],
        scratch_shapes=[pltpu.VMEM((tm, d), BF16)],
        compiler_params=_cparams(2),
        name="cross_attention",
    )(h1, x1, kvm, w_q, w_o, g.reshape(1, d))


FFN_TN = 512


def _ffn_up_kernel(h_ref, wg_ref, wu_ref, o_ref):
    h = h_ref[...]
    gt = jnp.dot(h, wg_ref[...], preferred_element_type=F32)
    up = jnp.dot(h, wu_ref[...], preferred_element_type=F32)
    o_ref[...] = (gt * _sigmoid(gt) * up).astype(o_ref.dtype)


def ffn_up(h, w_gate, w_up, tm):
    m, k = h.shape
    n = w_gate.shape[1]
    wspec = pl.BlockSpec((k, FFN_TN), lambda j, i: (0, j))
    return pl.pallas_call(
        _ffn_up_kernel,
        grid=(n // FFN_TN, m // tm),
        in_specs=[pl.BlockSpec((tm, k), lambda j, i: (i, 0)), wspec, wspec],
        out_specs=pl.BlockSpec((tm, FFN_TN), lambda j, i: (i, j)),
        out_shape=jax.ShapeDtypeStruct((m, n), BF16),
        compiler_params=_cparams(2),
        name="ffn_up",
    )(h, w_gate, w_up)


def _ffn_down_kernel(a_ref, w_ref, x_ref, g_ref, o_ref):
    x3 = x_ref[...] + jnp.dot(a_ref[...], w_ref[...], preferred_element_type=F32)
    o_ref[...] = _rms(x3, g_ref[...])


def ffn_down(a, w, x, g, tm):
    m, kdim = a.shape
    d = w.shape[1]
    return pl.pallas_call(
        _ffn_down_kernel,
        grid=(m // tm,),
        in_specs=[pl.BlockSpec((tm, kdim), lambda i: (i, 0)),
                  _resident((kdim, d)),
                  pl.BlockSpec((tm, d), lambda i: (i, 0)),
                  _resident((1, d))],
        out_specs=pl.BlockSpec((tm, d), lambda i: (i, 0)),
        out_shape=jax.ShapeDtypeStruct((m, d), F32),
        compiler_params=_cparams(1),
        name="ffn_down",
    )(a, w, x, g.reshape(1, d))


def _swap_halves(w):
    half = w.shape[-1] // 2
    return jnp.concatenate([w[..., half:], w[..., :half]], axis=-1)


def _rope_tables(positions):
    inv_freq = 1.0 / (ROPE_THETA ** (jnp.arange(0, QK_ROPE, 2, dtype=F32) / QK_ROPE))
    ang = positions.astype(F32).reshape(-1, 1) * inv_freq
    cos, sin = jnp.cos(ang), jnp.sin(ang)
    tab_k = jnp.concatenate([cos, cos, -sin, sin], axis=-1)
    c0 = (QK_NOPE + QK_ROPE) ** -0.5 * math.log2(math.e)
    cos_t, sin_t = cos.T, sin.T
    tab_qt = c0 * jnp.concatenate(
        [jnp.ones((QK_NOPE, ang.shape[0]), F32), cos_t, cos_t, -sin_t, sin_t], axis=0)
    return tab_k, tab_qt


def kernel(x, mem, positions, norm_mix, w_in, b_conv_in, norm_cq, w_uq, norm_ckv, w_ukv, w_dw, b_dw, ln_conv_g, ln_conv_b, w_pw2, b_pw2, w_o, norm_cross, norm_mem, w_q_mem, w_kv_mem, w_o_mem, norm_ffn, w_gate, w_up, w_down, norm_final):
    batch, seq, d = x.shape
    mem_len = mem.shape[1]
    t = batch * seq
    xf = x.reshape(t, d)
    tab_k, tab_qt = _rope_tables(positions)
    l = 0

    wi = w_in[l]
    o_kr = Q_LORA + KV_LORA
    o_u = o_kr + QK_ROPE
    o_g = o_u + 2 * CONV_CH
    w_lat = jnp.concatenate(
        [wi[:, :o_u], _swap_halves(wi[:, o_kr:o_u])], axis=1).astype(BF16)
    w_u = wi[:, o_u:o_g].astype(BF16)
    w_g = wi[:, o_g:].astype(BF16)
    wq3 = w_uq[l].reshape(Q_LORA, MLA_HEADS, QK_NOPE + QK_ROPE)
    w_qt = jnp.concatenate(
        [wq3, _swap_halves(wq3[..., QK_NOPE:])], axis=-1).reshape(Q_LORA, -1).T.astype(BF16)
    wkv3 = w_ukv[l].reshape(KV_LORA, MLA_HEADS, QK_NOPE + V_HEAD)
    w_k = wkv3[..., :QK_NOPE].reshape(KV_LORA, -1).astype(BF16)
    w_vt = wkv3[..., QK_NOPE:].reshape(KV_LORA, -1).T.astype(BF16)

    lat, u, sg = in_proj(xf, norm_mix[l], w_lat, w_u, b_conv_in[l], w_g, tm=256)
    qt = proj_q(lat, norm_cq[l], w_qt, tab_qt, tm=512)
    k, vt = proj_kv(lat, norm_ckv[l], w_k, w_vt, tab_k, tm=512)
    y_a = attention(qt, k, vt, batch, seq)
    cv = conv_branch(u, w_dw[l], b_dw[l], ln_conv_g[l], ln_conv_b[l], batch, seq, tm=256)
    x1, h1 = mix(cv, y_a, sg, xf, w_pw2[l].astype(BF16), b_pw2[l], w_o[l].astype(BF16),
                 norm_cross[l], tm=256)

    kvm = mem_kv(mem.reshape(batch * mem_len, d), norm_mem[l], w_kv_mem[l].astype(BF16), tn=1024)
    x2, h2 = cross_attention(h1, x1, kvm, w_q_mem[l].astype(BF16), w_o_mem[l].astype(BF16),
                             norm_ffn[l], batch, seq, mem_len, tm=256)

    a = ffn_up(h2, w_gate[l].astype(BF16), w_up[l].astype(BF16), tm=1024)
    out = ffn_down(a, w_down[l].astype(BF16), x2, norm_final, tm=256)
    return out.reshape(batch, seq, d)
```

```python
import functools
import math

import jax
import jax.numpy as jnp
from jax import lax
from jax.experimental import pallas as pl
from jax.experimental.pallas import tpu as pltpu

F32 = jnp.float32
BF16 = jnp.bfloat16

D_MODEL = 2048
CHUNK = 64
MLA_HEADS = 16
QK_NOPE = 128
QK_ROPE = 64
V_HEAD = 128
Q_LORA = 512
KV_LORA = 256
ROPE_THETA = 10000.0
CONV_CH = 1024
CONV_WIDTH = 31
MEM_HEADS = 4
MEM_HEAD_DIM = D_MODEL // MEM_HEADS
EPS = 1e-6
NEG_INF = -1e30

LANES = 128
SUBLANES = 8
HEAD_PAD = 256
BF16_ROWS = 16
VT_ROWS = V_HEAD + BF16_ROWS
LAT_W = Q_LORA + KV_LORA + 2 * QK_ROPE
HALO = 32
VMEM_LIMIT = 56 * 1024 * 1024

_NT = (((1,), (1,)), ((), ()))


def _cparams(n_axes):
    return pltpu.CompilerParams(
        dimension_semantics=("arbitrary",) * n_axes,
        vmem_limit_bytes=VMEM_LIMIT)


def _sigmoid(x):
    return 1.0 / (1.0 + jnp.exp(-x))


def _rms(x, g):
    ms = jnp.mean(x * x, axis=-1, keepdims=True)
    return x * lax.rsqrt(ms + EPS) * g


def _resident(shape):
    zeros = (0,) * len(shape)
    return pl.BlockSpec(shape, lambda *_: zeros, pipeline_mode=pl.Buffered(1))


W_BLK = 1024
N_WBLK = 7
CONV_GROUPS = 2


def _conv_ln_swish(ext_ref, sh_ref, w_ref, bdw_ref, g_ref, b_ref, o_ref, tm):
    n_sh = tm + HALO - SUBLANES
    for r in range(1, SUBLANES):
        sh_ref[r - 1, :, :] = ext_ref[r:r + n_sh, :]
    base = HALO - (CONV_WIDTH - 1)
    taps = sorted(range(CONV_WIDTH), key=lambda j: ((base + j) % SUBLANES, j))
    for c in range(tm // (SUBLANES * CONV_GROUPS)):
        r0 = c * SUBLANES * CONV_GROUPS
        accs = [bdw_ref[...]] * CONV_GROUPS
        for j in taps:
            phase = (base + j) % SUBLANES
            wj = w_ref[j]
            for gi in range(CONV_GROUPS):
                a0 = r0 + gi * SUBLANES + base + j - phase
                if phase == 0:
                    src = ext_ref[a0:a0 + SUBLANES, :]
                else:
                    src = sh_ref[phase - 1, a0:a0 + SUBLANES, :]
                accs[gi] = accs[gi] + wj * src
        for gi in range(CONV_GROUPS):
            acc = accs[gi]
            mu = jnp.mean(acc, axis=-1, keepdims=True)
            d = acc - mu
            var = jnp.mean(d * d, axis=-1, keepdims=True)
            y = d * lax.rsqrt(var + EPS) * g_ref[...] + b_ref[...]
            rows = slice(r0 + gi * SUBLANES, r0 + (gi + 1) * SUBLANES)
            o_ref[rows, :] = (y * _sigmoid(y)).astype(o_ref.dtype)


def _inproj_kernel(x_ref, g_ref, w0, w1, w2, w3, w4, w5, w6, bu_ref,
                   cw_ref, cb_ref, lg_ref, lb_ref,
                   lat_ref, sg_ref, cv_ref, ext_ref, unew_ref, sh_ref, *, tm, tiles_per_seq):
    i = pl.program_id(0)

    @pl.when(i == 0)
    def _():
        ext_ref[...] = jnp.zeros(ext_ref.shape, F32)

    _conv_ln_swish(ext_ref, sh_ref, cw_ref, cb_ref, lg_ref, lb_ref, cv_ref, tm)

    h = _rms(x_ref[...], g_ref[...]).astype(BF16)
    lat_ref[...] = jnp.dot(h, w0[...], preferred_element_type=F32)
    a = jnp.dot(h, w1[...], preferred_element_type=F32) + bu_ref[:, :CONV_CH]
    gt = jnp.dot(h, w2[...], preferred_element_type=F32) + bu_ref[:, CONV_CH:]
    unew_ref[...] = a * _sigmoid(gt)
    for c, w in enumerate((w3, w4, w5, w6)):
        acc = jnp.dot(h, w[...], preferred_element_type=F32)
        sg_ref[:, c * W_BLK:(c + 1) * W_BLK] = _sigmoid(acc).astype(sg_ref.dtype)

    @pl.when(i % tiles_per_seq == 0)
    def _():
        ext_ref[0:HALO, :] = jnp.zeros((HALO, CONV_CH), F32)

    @pl.when(i % tiles_per_seq != 0)
    def _():
        ext_ref[0:HALO, :] = ext_ref[tm:tm + HALO, :]

    ext_ref[HALO:, :] = unew_ref[...]


def in_proj_conv(x, g, w_all, b_u, w_dw, b_dw, ln_g, ln_b, seq, tm):
    m, d = x.shape
    n = m // tm
    cur = lambda w: pl.BlockSpec((tm, w), lambda i: (jnp.minimum(i, n - 1), 0))
    prev = lambda w: pl.BlockSpec((tm, w), lambda i: (jnp.maximum(i - 1, 0), 0))
    wblk = lambda c: pl.BlockSpec((d, W_BLK), lambda i: (0, c), pipeline_mode=pl.Buffered(1))
    w8 = jnp.broadcast_to(w_dw[:, None, :], (CONV_WIDTH, SUBLANES, CONV_CH))
    vec = lambda a: jnp.broadcast_to(a.reshape(1, CONV_CH), (SUBLANES, CONV_CH))
    return pl.pallas_call(
        functools.partial(_inproj_kernel, tm=tm, tiles_per_seq=seq // tm),
        grid=(n + 1,),
        in_specs=[cur(d), _resident((1, d))] + [wblk(c) for c in range(N_WBLK)]
                 + [_resident((1, 2 * CONV_CH)), _resident((CONV_WIDTH, SUBLANES, CONV_CH)),
                    _resident((SUBLANES, CONV_CH)), _resident((SUBLANES, CONV_CH)),
                    _resident((SUBLANES, CONV_CH))],
        out_specs=[cur(W_BLK), cur(4 * W_BLK), prev(CONV_CH)],
        out_shape=[jax.ShapeDtypeStruct((m, W_BLK), F32),
                   jax.ShapeDtypeStruct((m, 4 * W_BLK), BF16),
                   jax.ShapeDtypeStruct((m, CONV_CH), BF16)],
        scratch_shapes=[pltpu.VMEM((tm + HALO, CONV_CH), F32),
                        pltpu.VMEM((tm, CONV_CH), F32),
                        pltpu.VMEM((SUBLANES - 1, tm + HALO - SUBLANES, CONV_CH), F32)],
        compiler_params=_cparams(1),
        name="in_proj_conv",
    )(x, g.reshape(1, d), *([w_all] * N_WBLK), b_u.reshape(1, 2 * CONV_CH),
      w8, vec(b_dw), vec(ln_g), vec(ln_b))


def _qkv_kernel(cq_ref, gq_ref, wqt_ref, tabq_ref, c_ref, g_ref, kl_ref, tab_ref, wk_ref,
                wvt_ref, qt_ref, k_ref, vt_ref):
    cqn = _rms(cq_ref[...], gq_ref[...]).astype(BF16)
    tabq = tabq_ref[...]
    for hh in range(MLA_HEADS):
        rows = slice(hh * HEAD_PAD, (hh + 1) * HEAD_PAD)
        acc = lax.dot_general(wqt_ref[rows, :], cqn, _NT, preferred_element_type=F32)
        qt_ref[rows, :] = (acc * tabq).astype(qt_ref.dtype)

    cn = _rms(c_ref[...], g_ref[...]).astype(BF16)
    prod = kl_ref[...] * tab_ref[...]
    krot2 = (prod + pltpu.roll(prod, QK_ROPE, 1)).astype(k_ref.dtype)
    for pair in range(MLA_HEADS // 2):
        acc = jnp.dot(cn, wk_ref[:, pair * 256:(pair + 1) * 256],
                      preferred_element_type=F32)
        for sub in range(2):
            hh = 2 * pair + sub
            k_ref[:, hh * HEAD_PAD:hh * HEAD_PAD + QK_NOPE] = (
                acc[:, sub * QK_NOPE:(sub + 1) * QK_NOPE].astype(k_ref.dtype))
            k_ref[:, hh * HEAD_PAD + QK_NOPE:(hh + 1) * HEAD_PAD] = krot2
    vt = lax.dot_general(wvt_ref[...], cn, _NT, preferred_element_type=F32).astype(vt_ref.dtype)
    ones = jnp.ones((BF16_ROWS, vt.shape[1]), vt_ref.dtype)
    for hh in range(MLA_HEADS):
        vt_ref[hh * VT_ROWS:hh * VT_ROWS + V_HEAD, :] = vt[hh * V_HEAD:(hh + 1) * V_HEAD, :]
        vt_ref[hh * VT_ROWS + V_HEAD:(hh + 1) * VT_ROWS, :] = ones


def proj_qkv(lat, g_q, w_qt, tab_qt, g_kv, w_k, w_vt, tab_k, tm):
    m = lat.shape[0]
    ckv_blk = Q_LORA // KV_LORA
    kl_blk = (Q_LORA + KV_LORA) // LANES
    nq, nk, nv = MLA_HEADS * HEAD_PAD, MLA_HEADS * HEAD_PAD, MLA_HEADS * VT_ROWS
    return pl.pallas_call(
        _qkv_kernel,
        grid=(m // tm,),
        in_specs=[pl.BlockSpec((tm, Q_LORA), lambda i: (i, 0)),
                  _resident((1, Q_LORA)),
                  _resident(w_qt.shape),
                  pl.BlockSpec((HEAD_PAD, tm), lambda i: (0, i)),
                  pl.BlockSpec((tm, KV_LORA), lambda i: (i, ckv_blk)),
                  _resident((1, KV_LORA)),
                  pl.BlockSpec((tm, LANES), lambda i: (i, kl_blk)),
                  pl.BlockSpec((tm, LANES), lambda i: (i, 0)),
                  _resident(w_k.shape), _resident(w_vt.shape)],
        out_specs=[pl.BlockSpec((nq, tm), lambda i: (0, i)),
                   pl.BlockSpec((tm, nk), lambda i: (i, 0)),
                   pl.BlockSpec((nv, tm), lambda i: (0, i))],
        out_shape=[jax.ShapeDtypeStruct((nq, m), BF16),
                   jax.ShapeDtypeStruct((m, nk), BF16),
                   jax.ShapeDtypeStruct((nv, m), BF16)],
        compiler_params=_cparams(1),
        name="proj_qkv",
    )(lat, g_q.reshape(1, Q_LORA), w_qt, tab_qt,
      lat, g_kv.reshape(1, KV_LORA), lat, tab_k, w_k, w_vt)


ATT_TQ = 512
ATT_HEADS = 2
ATT_LOOKAHEAD = 2


def _attn_kernel(qt_ref, k_ref, vt_ref, o_ref, *, seq):
    tq = ATT_TQ
    g_diag = tq // SUBLANES
    key_chunk = lax.broadcasted_iota(jnp.int32, (g_diag, SUBLANES, tq), 0) // (CHUNK // SUBLANES)
    qry_chunk = lax.broadcasted_iota(jnp.int32, (g_diag, SUBLANES, tq), 2) // CHUNK
    diag_mask = key_chunk <= qry_chunk

    def scores(hh, qi):
        kend = (qi + 1) * tq
        s = jnp.dot(k_ref[0:kend, hh * HEAD_PAD:(hh + 1) * HEAD_PAD],
                    qt_ref[hh * HEAD_PAD:(hh + 1) * HEAD_PAD, qi * tq:kend],
                    preferred_element_type=F32)
        return s.reshape(kend // SUBLANES, SUBLANES, tq)

    stages = [(hh, qi) for qi in range(seq // tq) for hh in range(ATT_HEADS)]
    pending = [scores(*st) for st in stages[:ATT_LOOKAHEAD]]
    for idx, (hh, qi) in enumerate(stages):
        q0, kend = qi * tq, (qi + 1) * tq
        s = pending.pop(0)
        if idx + ATT_LOOKAHEAD < len(stages):
            pending.append(scores(*stages[idx + ATT_LOOKAHEAD]))
        g_past = q0 // SUBLANES
        s_d = jnp.where(diag_mask, s[g_past:], NEG_INF)
        m8 = jnp.max(s_d, axis=0)
        if qi > 0:
            m8 = jnp.maximum(m8, jnp.max(s[:g_past], axis=0))
        m8 = jnp.broadcast_to(jnp.max(m8, axis=0, keepdims=True), (SUBLANES, tq))
        p = jnp.exp2(s_d - m8[None])
        if qi > 0:
            p = jnp.concatenate([jnp.exp2(s[:g_past] - m8[None]), p], axis=0)
        p = p.reshape(kend, tq).astype(BF16)
        ot = jnp.dot(vt_ref[hh * VT_ROWS:(hh + 1) * VT_ROWS, 0:kend], p,
                     preferred_element_type=F32)
        inv = 1.0 / ot[V_HEAD:V_HEAD + 1, :]
        o_ref[q0:kend, hh * V_HEAD:(hh + 1) * V_HEAD] = (
            (ot[:V_HEAD, :] * inv).T.astype(o_ref.dtype))


def attention(qt, k, vt, batch, seq):
    nh = ATT_HEADS
    return pl.pallas_call(
        functools.partial(_attn_kernel, seq=seq),
        grid=(batch, MLA_HEADS // nh),
        in_specs=[pl.BlockSpec((nh * HEAD_PAD, seq), lambda b, h: (h, b)),
                  pl.BlockSpec((seq, nh * HEAD_PAD), lambda b, h: (b, h)),
                  pl.BlockSpec((nh * VT_ROWS, seq), lambda b, h: (h, b))],
        out_specs=pl.BlockSpec((seq, nh * V_HEAD), lambda b, h: (b, h)),
        out_shape=jax.ShapeDtypeStruct((batch * seq, MLA_HEADS * V_HEAD), BF16),
        compiler_params=_cparams(2),
        name="mla_attention",
    )(qt, k, vt)


def _mix_kernel(v_ref, ya_ref, sga_ref, sgb_ref, x_ref, wpw_ref, bpw_ref, wo_ref, g_ref,
                x1_ref, h1_ref):
    yb = jnp.dot(v_ref[...], wpw_ref[...], preferred_element_type=F32) + bpw_ref[...]
    merged = (sga_ref[...].astype(F32) * ya_ref[...].astype(F32)
              + sgb_ref[...].astype(F32) * yb)
    x1 = x_ref[...] + jnp.dot(merged.astype(BF16), wo_ref[...], preferred_element_type=F32)
    x1_ref[...] = x1
    h1_ref[...] = _rms(x1, g_ref[...]).astype(h1_ref.dtype)


def mix(v, ya, sg, x, w_pw2, b_pw2, w_o, g, tm):
    m, d = x.shape
    row = lambda w: pl.BlockSpec((tm, w), lambda i: (i, 0))
    return pl.pallas_call(
        _mix_kernel,
        grid=(m // tm,),
        in_specs=[row(CONV_CH), row(d),
                  pl.BlockSpec((tm, d), lambda i: (i, 0)),
                  pl.BlockSpec((tm, d), lambda i: (i, 1)),
                  row(d),
                  _resident((CONV_CH, d)), _resident((1, d)), _resident((d, d)),
                  _resident((1, d))],
        out_specs=[row(d), row(d)],
        out_shape=[jax.ShapeDtypeStruct((m, d), F32), jax.ShapeDtypeStruct((m, d), BF16)],
        compiler_params=_cparams(1),
        name="mix_out_proj",
    )(v, ya, sg, sg, x, w_pw2, b_pw2.reshape(1, d), w_o, g.reshape(1, d))


def _memkv_kernel(m_ref, g_ref, w_ref, o_ref, mn_ref):
    @pl.when(pl.program_id(0) == 0)
    def _():
        mn_ref[...] = _rms(m_ref[...], g_ref[...]).astype(mn_ref.dtype)

    o_ref[...] = jnp.dot(mn_ref[...], w_ref[...], preferred_element_type=F32).astype(o_ref.dtype)


def mem_kv(mem, g, w, tn):
    m, k = mem.shape
    n = w.shape[1]
    return pl.pallas_call(
        _memkv_kernel,
        grid=(n // tn,),
        in_specs=[_resident((m, k)), _resident((1, k)),
                  pl.BlockSpec((k, tn), lambda j: (0, j))],
        out_specs=pl.BlockSpec((m, tn), lambda j: (0, j)),
        out_shape=jax.ShapeDtypeStruct((m, n), BF16),
        scratch_shapes=[pltpu.VMEM((m, k), BF16)],
        compiler_params=_cparams(1),
        name="mem_kv",
    )(mem, g.reshape(1, k), w)


def _cross_kernel(h_ref, x_ref, kv_ref, wq_ref, wo_ref, g_ref, x2_ref, h2_ref, o_scr):
    qm = jnp.dot(h_ref[...], wq_ref[...], preferred_element_type=F32)
    scale = MEM_HEAD_DIM ** -0.5
    for hd in range(MEM_HEADS):
        sl = slice(hd * MEM_HEAD_DIM, (hd + 1) * MEM_HEAD_DIM)
        vsl = slice(D_MODEL + hd * MEM_HEAD_DIM, D_MODEL + (hd + 1) * MEM_HEAD_DIM)
        s = lax.dot_general(qm[:, sl].astype(BF16), kv_ref[:, sl], _NT,
                            preferred_element_type=F32) * scale
        m = jnp.max(s, axis=-1, keepdims=True)
        p = jnp.exp(s - m)
        l = jnp.sum(p, axis=-1, keepdims=True)
        o = jnp.dot(p.astype(BF16), kv_ref[:, vsl], preferred_element_type=F32)
        o_scr[:, sl] = (o * (1.0 / l)).astype(o_scr.dtype)
    x2 = x_ref[...] + jnp.dot(o_scr[...], wo_ref[...], preferred_element_type=F32)
    x2_ref[...] = x2
    h2_ref[...] = _rms(x2, g_ref[...]).astype(h2_ref.dtype)


def cross_attention(h1, x1, kvm, w_q, w_o, g, batch, seq, mem_len, tm):
    d = D_MODEL
    nt = seq // tm
    row = pl.BlockSpec((tm, d), lambda b, s: (b * nt + s, 0))
    return pl.pallas_call(
        _cross_kernel,
        grid=(batch, nt),
        in_specs=[row, row,
                  pl.BlockSpec((mem_len, 2 * d), lambda b, s: (b, 0)),
                  _resident((d, d)), _resident((d, d)), _resident((1, d))],
        out_specs=[row, row],
        out_shape=[jax.ShapeDtypeStruct((batch * seq, d), F32),
                   jax.ShapeDtypeStruct((batch * seq, d), BF16)],
        scratch_shapes=[pltpu.VMEM((tm, d), BF16)],
        compiler_params=_cparams(2),
        name="cross_attention",
    )(h1, x1, kvm, w_q, w_o, g.reshape(1, d))


FFN_TN = 512


def _ffn_up_kernel(h_ref, wg_ref, wu_ref, o_ref):
    h = h_ref[...]
    gt = jnp.dot(h, wg_ref[...], preferred_element_type=F32)
    up = jnp.dot(h, wu_ref[...], preferred_element_type=F32)
    o_ref[...] = (gt * _sigmoid(gt) * up).astype(o_ref.dtype)


def ffn_up(h, w_gate, w_up, tm):
    m, k = h.shape
    n = w_gate.shape[1]
    wspec = pl.BlockSpec((k, FFN_TN), lambda j, i: (0, j))
    return pl.pallas_call(
        _ffn_up_kernel,
        grid=(n // FFN_TN, m // tm),
        in_specs=[pl.BlockSpec((tm, k), lambda j, i: (i, 0)), wspec, wspec],
        out_specs=pl.BlockSpec((tm, FFN_TN), lambda j, i: (i, j)),
        out_shape=jax.ShapeDtypeStruct((m, n), BF16),
        compiler_params=_cparams(2),
        name="ffn_up",
    )(h, w_gate, w_up)


def _ffn_down_kernel(a_ref, w_ref, x_ref, g_ref, o_ref):
    x3 = x_ref[...] + jnp.dot(a_ref[...], w_ref[...], preferred_element_type=F32)
    o_ref[...] = _rms(x3, g_ref[...])


def ffn_down(a, w, x, g, tm):
    m, kdim = a.shape
    d = w.shape[1]
    return pl.pallas_call(
        _ffn_down_kernel,
        grid=(m // tm,),
        in_specs=[pl.BlockSpec((tm, kdim), lambda i: (i, 0)),
                  _resident((kdim, d)),
                  pl.BlockSpec((tm, d), lambda i: (i, 0)),
                  _resident((1, d))],
        out_specs=pl.BlockSpec((tm, d), lambda i: (i, 0)),
        out_shape=jax.ShapeDtypeStruct((m, d), F32),
        compiler_params=_cparams(1),
        name="ffn_down",
    )(a, w, x, g.reshape(1, d))


def _swap_halves(w):
    half = w.shape[-1] // 2
    return jnp.concatenate([w[..., half:], w[..., :half]], axis=-1)


def _rope_tables(positions):
    inv_freq = 1.0 / (ROPE_THETA ** (jnp.arange(0, QK_ROPE, 2, dtype=F32) / QK_ROPE))
    ang = positions.astype(F32).reshape(-1, 1) * inv_freq
    cos, sin = jnp.cos(ang), jnp.sin(ang)
    tab_k = jnp.concatenate([cos, cos, -sin, sin], axis=-1)
    c0 = (QK_NOPE + QK_ROPE) ** -0.5 * math.log2(math.e)
    cos_t, sin_t = cos.T, sin.T
    tab_qt = c0 * jnp.concatenate(
        [jnp.ones((QK_NOPE, ang.shape[0]), F32), cos_t, cos_t, -sin_t, sin_t], axis=0)
    return tab_k, tab_qt


def kernel(x, mem, positions, norm_mix, w_in, b_conv_in, norm_cq, w_uq, norm_ckv, w_ukv, w_dw, b_dw, ln_conv_g, ln_conv_b, w_pw2, b_pw2, w_o, norm_cross, norm_mem, w_q_mem, w_kv_mem, w_o_mem, norm_ffn, w_gate, w_up, w_down, norm_final):
    batch, seq, d = x.shape
    mem_len = mem.shape[1]
    t = batch * seq
    xf = x.reshape(t, d)
    tab_k, tab_qt = _rope_tables(positions)
    l = 0

    wi = w_in[l]
    o_kr = Q_LORA + KV_LORA
    o_u = o_kr + QK_ROPE
    o_g = o_u + 2 * CONV_CH
    w_all = jnp.concatenate(
        [wi[:, :o_u], _swap_halves(wi[:, o_kr:o_u]), jnp.zeros((d, W_BLK - LAT_W), F32),
         wi[:, o_u:]], axis=1).astype(BF16)
    wq3 = w_uq[l].reshape(Q_LORA, MLA_HEADS, QK_NOPE + QK_ROPE)
    w_qt = jnp.concatenate(
        [wq3, _swap_halves(wq3[..., QK_NOPE:])], axis=-1).reshape(Q_LORA, -1).T.astype(BF16)
    wkv3 = w_ukv[l].reshape(KV_LORA, MLA_HEADS, QK_NOPE + V_HEAD)
    w_k = wkv3[..., :QK_NOPE].reshape(KV_LORA, -1).astype(BF16)
    w_vt = wkv3[..., QK_NOPE:].reshape(KV_LORA, -1).T.astype(BF16)

    lat, sg, cv = in_proj_conv(xf, norm_mix[l], w_all, b_conv_in[l], w_dw[l], b_dw[l],
                               ln_conv_g[l], ln_conv_b[l], seq, tm=256)
    qt, k, vt = proj_qkv(lat, norm_cq[l], w_qt, tab_qt, norm_ckv[l], w_k, w_vt, tab_k, tm=512)
    y_a = attention(qt, k, vt, batch, seq)
    x1, h1 = mix(cv, y_a, sg, xf, w_pw2[l].astype(BF16), b_pw2[l], w_o[l].astype(BF16),
                 norm_cross[l], tm=256)

    kvm = mem_kv(mem.reshape(batch * mem_len, d), norm_mem[l], w_kv_mem[l].astype(BF16), tn=1024)
    x2, h2 = cross_attention(h1, x1, kvm, w_q_mem[l].astype(BF16), w_o_mem[l].astype(BF16),
                             norm_ffn[l], batch, seq, mem_len, tm=256)

    a = ffn_up(h2, w_gate[l].astype(BF16), w_up[l].astype(BF16), tm=1024)
    out = ffn_down(a, w_down[l].astype(BF16), x2, norm_final, tm=256)
    return out.reshape(batch, seq, d)
```

```python
import functools
import math

import jax
import jax.numpy as jnp
from jax import lax
from jax.experimental import pallas as pl
from jax.experimental.pallas import tpu as pltpu

F32 = jnp.float32
BF16 = jnp.bfloat16

D_MODEL = 2048
CHUNK = 64
MLA_HEADS = 16
QK_NOPE = 128
QK_ROPE = 64
V_HEAD = 128
Q_LORA = 512
KV_LORA = 256
ROPE_THETA = 10000.0
CONV_CH = 1024
CONV_WIDTH = 31
MEM_HEADS = 4
MEM_HEAD_DIM = D_MODEL // MEM_HEADS
EPS = 1e-6
NEG_INF = -1e30

LANES = 128
SUBLANES = 8
HEAD_PAD = 256
BF16_ROWS = 16
VT_ROWS = V_HEAD + BF16_ROWS
LAT_W = Q_LORA + KV_LORA + 2 * QK_ROPE
HALO = 32
VMEM_LIMIT = 56 * 1024 * 1024

_NT = (((1,), (1,)), ((), ()))


def _cparams(n_axes):
    return pltpu.CompilerParams(
        dimension_semantics=("arbitrary",) * n_axes,
        vmem_limit_bytes=VMEM_LIMIT)


def _sigmoid(x):
    return 1.0 / (1.0 + jnp.exp(-x))


def _rms(x, g):
    ms = jnp.mean(x * x, axis=-1, keepdims=True)
    return x * lax.rsqrt(ms + EPS) * g


def _resident(shape):
    zeros = (0,) * len(shape)
    return pl.BlockSpec(shape, lambda *_: zeros, pipeline_mode=pl.Buffered(1))


W_BLK = 1024
N_WBLK = 7
CONV_GROUPS = 2


def _conv_ln_swish(ext_ref, sh_ref, w_ref, bdw_ref, g_ref, b_ref, o_ref, tm):
    n_sh = tm + HALO - SUBLANES
    for r in range(1, SUBLANES):
        sh_ref[r - 1, :, :] = ext_ref[r:r + n_sh, :]
    base = HALO - (CONV_WIDTH - 1)
    taps = sorted(range(CONV_WIDTH), key=lambda j: ((base + j) % SUBLANES, j))
    for c in range(tm // (SUBLANES * CONV_GROUPS)):
        r0 = c * SUBLANES * CONV_GROUPS
        accs = [bdw_ref[...]] * CONV_GROUPS
        for j in taps:
            phase = (base + j) % SUBLANES
            wj = w_ref[j]
            for gi in range(CONV_GROUPS):
                a0 = r0 + gi * SUBLANES + base + j - phase
                if phase == 0:
                    src = ext_ref[a0:a0 + SUBLANES, :]
                else:
                    src = sh_ref[phase - 1, a0:a0 + SUBLANES, :]
                accs[gi] = accs[gi] + wj * src
        for gi in range(CONV_GROUPS):
            acc = accs[gi]
            mu = jnp.mean(acc, axis=-1, keepdims=True)
            d = acc - mu
            var = jnp.mean(d * d, axis=-1, keepdims=True)
            y = d * lax.rsqrt(var + EPS) * g_ref[...] + b_ref[...]
            rows = slice(r0 + gi * SUBLANES, r0 + (gi + 1) * SUBLANES)
            o_ref[rows, :] = (y * _sigmoid(y)).astype(o_ref.dtype)


def _inproj_kernel(x_ref, g_ref, w0, w1, w2, w3, w4, w5, w6, bu_ref,
                   cw_ref, cb_ref, lg_ref, lb_ref,
                   lat_ref, sg_ref, cv_ref, ext_ref, unew_ref, sh_ref, *, tm, tiles_per_seq):
    i = pl.program_id(0)

    @pl.when(i == 0)
    def _():
        ext_ref[...] = jnp.zeros(ext_ref.shape, F32)

    _conv_ln_swish(ext_ref, sh_ref, cw_ref, cb_ref, lg_ref, lb_ref, cv_ref, tm)

    h = _rms(x_ref[...], g_ref[...]).astype(BF16)
    lat_ref[...] = jnp.dot(h, w0[...], preferred_element_type=F32)
    a = jnp.dot(h, w1[...], preferred_element_type=F32) + bu_ref[:, :CONV_CH]
    gt = jnp.dot(h, w2[...], preferred_element_type=F32) + bu_ref[:, CONV_CH:]
    unew_ref[...] = a * _sigmoid(gt)
    for c, w in enumerate((w3, w4, w5, w6)):
        acc = jnp.dot(h, w[...], preferred_element_type=F32)
        sg_ref[:, c * W_BLK:(c + 1) * W_BLK] = _sigmoid(acc).astype(sg_ref.dtype)

    @pl.when(i % tiles_per_seq == 0)
    def _():
        ext_ref[0:HALO, :] = jnp.zeros((HALO, CONV_CH), F32)

    @pl.when(i % tiles_per_seq != 0)
    def _():
        ext_ref[0:HALO, :] = ext_ref[tm:tm + HALO, :]

    ext_ref[HALO:, :] = unew_ref[...]


def in_proj_conv(x, g, w_all, b_u, w_dw, b_dw, ln_g, ln_b, seq, tm):
    m, d = x.shape
    n = m // tm
    cur = lambda w: pl.BlockSpec((tm, w), lambda i: (jnp.minimum(i, n - 1), 0))
    prev = lambda w: pl.BlockSpec((tm, w), lambda i: (jnp.maximum(i - 1, 0), 0))
    wblk = lambda c: pl.BlockSpec((d, W_BLK), lambda i: (0, c), pipeline_mode=pl.Buffered(1))
    w8 = jnp.broadcast_to(w_dw[:, None, :], (CONV_WIDTH, SUBLANES, CONV_CH))
    vec = lambda a: jnp.broadcast_to(a.reshape(1, CONV_CH), (SUBLANES, CONV_CH))
    return pl.pallas_call(
        functools.partial(_inproj_kernel, tm=tm, tiles_per_seq=seq // tm),
        grid=(n + 1,),
        in_specs=[cur(d), _resident((1, d))] + [wblk(c) for c in range(N_WBLK)]
                 + [_resident((1, 2 * CONV_CH)), _resident((CONV_WIDTH, SUBLANES, CONV_CH)),
                    _resident((SUBLANES, CONV_CH)), _resident((SUBLANES, CONV_CH)),
                    _resident((SUBLANES, CONV_CH))],
        out_specs=[cur(W_BLK), cur(4 * W_BLK), prev(CONV_CH)],
        out_shape=[jax.ShapeDtypeStruct((m, W_BLK), F32),
                   jax.ShapeDtypeStruct((m, 4 * W_BLK), BF16),
                   jax.ShapeDtypeStruct((m, CONV_CH), BF16)],
        scratch_shapes=[pltpu.VMEM((tm + HALO, CONV_CH), F32),
                        pltpu.VMEM((tm, CONV_CH), F32),
                        pltpu.VMEM((SUBLANES - 1, tm + HALO - SUBLANES, CONV_CH), F32)],
        compiler_params=_cparams(1),
        name="in_proj_conv",
    )(x, g.reshape(1, d), *([w_all] * N_WBLK), b_u.reshape(1, 2 * CONV_CH),
      w8, vec(b_dw), vec(ln_g), vec(ln_b))


def _qkv_kernel(cq_ref, gq_ref, wqt_ref, tabq_ref, c_ref, g_ref, kl_ref, tab_ref, wk_ref,
                wvt_ref, qt_ref, k_ref, vt_ref):
    cqn = _rms(cq_ref[...], gq_ref[...]).astype(BF16)
    tabq = tabq_ref[...]
    for hh in range(MLA_HEADS):
        rows = slice(hh * HEAD_PAD, (hh + 1) * HEAD_PAD)
        acc = lax.dot_general(wqt_ref[rows, :], cqn, _NT, preferred_element_type=F32)
        qt_ref[rows, :] = (acc * tabq).astype(qt_ref.dtype)

    cn = _rms(c_ref[...], g_ref[...]).astype(BF16)
    prod = kl_ref[...] * tab_ref[...]
    krot2 = (prod + pltpu.roll(prod, QK_ROPE, 1)).astype(k_ref.dtype)
    for pair in range(MLA_HEADS // 2):
        acc = jnp.dot(cn, wk_ref[:, pair * 256:(pair + 1) * 256],
                      preferred_element_type=F32)
        for sub in range(2):
            hh = 2 * pair + sub
            k_ref[:, hh * HEAD_PAD:hh * HEAD_PAD + QK_NOPE] = (
                acc[:, sub * QK_NOPE:(sub + 1) * QK_NOPE].astype(k_ref.dtype))
            k_ref[:, hh * HEAD_PAD + QK_NOPE:(hh + 1) * HEAD_PAD] = krot2
    vt = lax.dot_general(wvt_ref[...], cn, _NT, preferred_element_type=F32).astype(vt_ref.dtype)
    ones = jnp.ones((BF16_ROWS, vt.shape[1]), vt_ref.dtype)
    for hh in range(MLA_HEADS):
        vt_ref[hh * VT_ROWS:hh * VT_ROWS + V_HEAD, :] = vt[hh * V_HEAD:(hh + 1) * V_HEAD, :]
        vt_ref[hh * VT_ROWS + V_HEAD:(hh + 1) * VT_ROWS, :] = ones


def proj_qkv(lat, g_q, w_qt, tab_qt, g_kv, w_k, w_vt, tab_k, tm):
    m = lat.shape[0]
    ckv_blk = Q_LORA // KV_LORA
    kl_blk = (Q_LORA + KV_LORA) // LANES
    nq, nk, nv = MLA_HEADS * HEAD_PAD, MLA_HEADS * HEAD_PAD, MLA_HEADS * VT_ROWS
    return pl.pallas_call(
        _qkv_kernel,
        grid=(m // tm,),
        in_specs=[pl.BlockSpec((tm, Q_LORA), lambda i: (i, 0)),
                  _resident((1, Q_LORA)),
                  _resident(w_qt.shape),
                  pl.BlockSpec((HEAD_PAD, tm), lambda i: (0, i)),
                  pl.BlockSpec((tm, KV_LORA), lambda i: (i, ckv_blk)),
                  _resident((1, KV_LORA)),
                  pl.BlockSpec((tm, LANES), lambda i: (i, kl_blk)),
                  pl.BlockSpec((tm, LANES), lambda i: (i, 0)),
                  _resident(w_k.shape), _resident(w_vt.shape)],
        out_specs=[pl.BlockSpec((nq, tm), lambda i: (0, i)),
                   pl.BlockSpec((tm, nk), lambda i: (i, 0)),
                   pl.BlockSpec((nv, tm), lambda i: (0, i))],
        out_shape=[jax.ShapeDtypeStruct((nq, m), BF16),
                   jax.ShapeDtypeStruct((m, nk), BF16),
                   jax.ShapeDtypeStruct((nv, m), BF16)],
        compiler_params=_cparams(1),
        name="proj_qkv",
    )(lat, g_q.reshape(1, Q_LORA), w_qt, tab_qt,
      lat, g_kv.reshape(1, KV_LORA), lat, tab_k, w_k, w_vt)


ATT_TQ = 256
ATT_HEADS = 2
ATT_LOOKAHEAD = 4


def _attn_kernel(qt_ref, k_ref, vt_ref, o_ref, *, seq):
    tq = ATT_TQ
    g_diag = tq // SUBLANES
    key_chunk = lax.broadcasted_iota(jnp.int32, (g_diag, SUBLANES, tq), 0) // (CHUNK // SUBLANES)
    qry_chunk = lax.broadcasted_iota(jnp.int32, (g_diag, SUBLANES, tq), 2) // CHUNK
    diag_mask = key_chunk <= qry_chunk

    def scores(hh, qi):
        kend = (qi + 1) * tq
        s = jnp.dot(k_ref[0:kend, hh * HEAD_PAD:(hh + 1) * HEAD_PAD],
                    qt_ref[hh * HEAD_PAD:(hh + 1) * HEAD_PAD, qi * tq:kend],
                    preferred_element_type=F32)
        return s.reshape(kend // SUBLANES, SUBLANES, tq)

    stages = [(hh, qi) for qi in range(seq // tq) for hh in range(ATT_HEADS)]
    pending = [scores(*st) for st in stages[:ATT_LOOKAHEAD]]
    for idx, (hh, qi) in enumerate(stages):
        q0, kend = qi * tq, (qi + 1) * tq
        s = pending.pop(0)
        if idx + ATT_LOOKAHEAD < len(stages):
            pending.append(scores(*stages[idx + ATT_LOOKAHEAD]))
        g_past = q0 // SUBLANES
        s_d = jnp.where(diag_mask, s[g_past:], NEG_INF)
        m8 = jnp.max(s_d, axis=0)
        if qi > 0:
            m8 = jnp.maximum(m8, jnp.max(s[:g_past], axis=0))
        m8 = jnp.broadcast_to(jnp.max(m8, axis=0, keepdims=True), (SUBLANES, tq))
        p = jnp.exp2(s_d - m8[None])
        if qi > 0:
            p = jnp.concatenate([jnp.exp2(s[:g_past] - m8[None]), p], axis=0)
        p = p.reshape(kend, tq).astype(BF16)
        ot = jnp.dot(vt_ref[hh * VT_ROWS:(hh + 1) * VT_ROWS, 0:kend], p,
                     preferred_element_type=F32)
        inv = 1.0 / ot[V_HEAD:V_HEAD + 1, :]
        o_ref[q0:kend, hh * V_HEAD:(hh + 1) * V_HEAD] = (
            (ot[:V_HEAD, :] * inv).T.astype(o_ref.dtype))


def attention(qt, k, vt, batch, seq):
    nh = ATT_HEADS
    return pl.pallas_call(
        functools.partial(_attn_kernel, seq=seq),
        grid=(batch, MLA_HEADS // nh),
        in_specs=[pl.BlockSpec((nh * HEAD_PAD, seq), lambda b, h: (h, b)),
                  pl.BlockSpec((seq, nh * HEAD_PAD), lambda b, h: (b, h)),
                  pl.BlockSpec((nh * VT_ROWS, seq), lambda b, h: (h, b))],
        out_specs=pl.BlockSpec((seq, nh * V_HEAD), lambda b, h: (b, h)),
        out_shape=jax.ShapeDtypeStruct((batch * seq, MLA_HEADS * V_HEAD), BF16),
        compiler_params=_cparams(2),
        name="mla_attention",
    )(qt, k, vt)


def _mix_kernel(v_ref, ya_ref, sga_ref, sgb_ref, x_ref, wpw_ref, bpw_ref, wo_ref, g_ref,
                x1_ref, h1_ref):
    yb = jnp.dot(v_ref[...], wpw_ref[...], preferred_element_type=F32) + bpw_ref[...]
    merged = (sga_ref[...].astype(F32) * ya_ref[...].astype(F32)
              + sgb_ref[...].astype(F32) * yb)
    x1 = x_ref[...] + jnp.dot(merged.astype(BF16), wo_ref[...], preferred_element_type=F32)
    x1_ref[...] = x1
    h1_ref[...] = _rms(x1, g_ref[...]).astype(h1_ref.dtype)


def mix(v, ya, sg, x, w_pw2, b_pw2, w_o, g, tm):
    m, d = x.shape
    row = lambda w: pl.BlockSpec((tm, w), lambda i: (i, 0))
    return pl.pallas_call(
        _mix_kernel,
        grid=(m // tm,),
        in_specs=[row(CONV_CH), row(d),
                  pl.BlockSpec((tm, d), lambda i: (i, 0)),
                  pl.BlockSpec((tm, d), lambda i: (i, 1)),
                  row(d),
                  _resident((CONV_CH, d)), _resident((1, d)), _resident((d, d)),
                  _resident((1, d))],
        out_specs=[row(d), row(d)],
        out_shape=[jax.ShapeDtypeStruct((m, d), F32), jax.ShapeDtypeStruct((m, d), BF16)],
        compiler_params=_cparams(1),
        name="mix_out_proj",
    )(v, ya, sg, sg, x, w_pw2, b_pw2.reshape(1, d), w_o, g.reshape(1, d))


def _memkv_kernel(m_ref, g_ref, w_ref, o_ref, mn_ref):
    @pl.when(pl.program_id(0) == 0)
    def _():
        mn_ref[...] = _rms(m_ref[...], g_ref[...]).astype(mn_ref.dtype)

    o_ref[...] = jnp.dot(mn_ref[...], w_ref[...], preferred_element_type=F32).astype(o_ref.dtype)


def mem_kv(mem, g, w, tn):
    m, k = mem.shape
    n = w.shape[1]
    return pl.pallas_call(
        _memkv_kernel,
        grid=(n // tn,),
        in_specs=[_resident((m, k)), _resident((1, k)),
                  pl.BlockSpec((k, tn), lambda j: (0, j))],
        out_specs=pl.BlockSpec((m, tn), lambda j: (0, j)),
        out_shape=jax.ShapeDtypeStruct((m, n), BF16),
        scratch_shapes=[pltpu.VMEM((m, k), BF16)],
        compiler_params=_cparams(1),
        name="mem_kv",
    )(mem, g.reshape(1, k), w)


def _cross_kernel(h_ref, x_ref, kv_ref, wq_ref, wo_ref, g_ref, x2_ref, h2_ref, o_scr):
    qm = jnp.dot(h_ref[...], wq_ref[...], preferred_element_type=F32)
    scale = MEM_HEAD_DIM ** -0.5

    def scores(hd):
        sl = slice(hd * MEM_HEAD_DIM, (hd + 1) * MEM_HEAD_DIM)
        return lax.dot_general(qm[:, sl].astype(BF16), kv_ref[:, sl], _NT,
                               preferred_element_type=F32) * scale

    nxt = scores(0)
    for hd in range(MEM_HEADS):
        sl = slice(hd * MEM_HEAD_DIM, (hd + 1) * MEM_HEAD_DIM)
        vsl = slice(D_MODEL + hd * MEM_HEAD_DIM, D_MODEL + (hd + 1) * MEM_HEAD_DIM)
        s = nxt
        if hd + 1 < MEM_HEADS:
            nxt = scores(hd + 1)
        m = jnp.max(s, axis=-1, keepdims=True)
        p = jnp.exp(s - m)
        l = jnp.sum(p, axis=-1, keepdims=True)
        o = jnp.dot(p.astype(BF16), kv_ref[:, vsl], preferred_element_type=F32)
        o_scr[:, sl] = (o * (1.0 / l)).astype(o_scr.dtype)
    x2 = x_ref[...] + jnp.dot(o_scr[...], wo_ref[...], preferred_element_type=F32)
    x2_ref[...] = x2
    h2_ref[...] = _rms(x2, g_ref[...]).astype(h2_ref.dtype)


def cross_attention(h1, x1, kvm, w_q, w_o, g, batch, seq, mem_len, tm):
    d = D_MODEL
    nt = seq // tm
    row = pl.BlockSpec((tm, d), lambda b, s: (b * nt + s, 0))
    return pl.pallas_call(
        _cross_kernel,
        grid=(batch, nt),
        in_specs=[row, row,
                  pl.BlockSpec((mem_len, 2 * d), lambda b, s: (b, 0)),
                  _resident((d, d)), _resident((d, d)), _resident((1, d))],
        out_specs=[row, row],
        out_shape=[jax.ShapeDtypeStruct((batch * seq, d), F32),
                   jax.ShapeDtypeStruct((batch * seq, d), BF16)],
        scratch_shapes=[pltpu.VMEM((tm, d), BF16)],
        compiler_params=_cparams(2),
        name="cross_attention",
    )(h1, x1, kvm, w_q, w_o, g.reshape(1, d))


FFN_TN = 512


def _ffn_up_kernel(h_ref, wg_ref, wu_ref, o_ref):
    h = h_ref[...]
    gt = jnp.dot(h, wg_ref[...], preferred_element_type=F32)
    up = jnp.dot(h, wu_ref[...], preferred_element_type=F32)
    o_ref[...] = (gt * _sigmoid(gt) * up).astype(o_ref.dtype)


def ffn_up(h, w_gate, w_up, tm):
    m, k = h.shape
    n = w_gate.shape[1]
    wspec = pl.BlockSpec((k, FFN_TN), lambda j, i: (0, j))
    return pl.pallas_call(
        _ffn_up_kernel,
        grid=(n // FFN_TN, m // tm),
        in_specs=[pl.BlockSpec((tm, k), lambda j, i: (i, 0)), wspec, wspec],
        out_specs=pl.BlockSpec((tm, FFN_TN), lambda j, i: (i, j)),
        out_shape=jax.ShapeDtypeStruct((m, n), BF16),
        compiler_params=_cparams(2),
        name="ffn_up",
    )(h, w_gate, w_up)


def _ffn_down_kernel(a_ref, w_ref, x_ref, g_ref, o_ref):
    x3 = x_ref[...] + jnp.dot(a_ref[...], w_ref[...], preferred_element_type=F32)
    o_ref[...] = _rms(x3, g_ref[...])


def ffn_down(a, w, x, g, tm):
    m, kdim = a.shape
    d = w.shape[1]
    return pl.pallas_call(
        _ffn_down_kernel,
        grid=(m // tm,),
        in_specs=[pl.BlockSpec((tm, kdim), lambda i: (i, 0)),
                  _resident((kdim, d)),
                  pl.BlockSpec((tm, d), lambda i: (i, 0)),
                  _resident((1, d))],
        out_specs=pl.BlockSpec((tm, d), lambda i: (i, 0)),
        out_shape=jax.ShapeDtypeStruct((m, d), F32),
        compiler_params=_cparams(1),
        name="ffn_down",
    )(a, w, x, g.reshape(1, d))


def _swap_halves(w):
    half = w.shape[-1] // 2
    return jnp.concatenate([w[..., half:], w[..., :half]], axis=-1)


def _pack_w_kernel(w_ref, rope_ref, o_ref):
    o_kr = Q_LORA + KV_LORA
    o_u = o_kr + QK_ROPE
    o_ref[:, 0:o_kr] = w_ref[:, 0:o_kr].astype(o_ref.dtype)
    o_ref[:, o_kr:LAT_W] = rope_ref[...]
    o_ref[:, LAT_W:W_BLK] = jnp.zeros((o_ref.shape[0], W_BLK - LAT_W), o_ref.dtype)
    o_ref[:, W_BLK:] = w_ref[:, o_u:].astype(o_ref.dtype)


def pack_w_in(wi, tr):
    d, n_in = wi.shape
    o_kr = Q_LORA + KV_LORA
    rope = wi[:, o_kr:o_kr + QK_ROPE]
    rope2 = jnp.concatenate([rope, _swap_halves(rope)], axis=1).astype(BF16)
    n_out = N_WBLK * W_BLK
    return pl.pallas_call(
        _pack_w_kernel,
        grid=(d // tr,),
        in_specs=[pl.BlockSpec((tr, n_in), lambda i: (i, 0)),
                  pl.BlockSpec((tr, 2 * QK_ROPE), lambda i: (i, 0))],
        out_specs=pl.BlockSpec((tr, n_out), lambda i: (i, 0)),
        out_shape=jax.ShapeDtypeStruct((d, n_out), BF16),
        compiler_params=_cparams(1),
        name="pack_w_in",
    )(wi, rope2)


def _rope_tables(positions):
    inv_freq = 1.0 / (ROPE_THETA ** (jnp.arange(0, QK_ROPE, 2, dtype=F32) / QK_ROPE))
    ang = positions.astype(F32).reshape(-1, 1) * inv_freq
    cos, sin = jnp.cos(ang), jnp.sin(ang)
    tab_k = jnp.concatenate([cos, cos, -sin, sin], axis=-1)
    c0 = (QK_NOPE + QK_ROPE) ** -0.5 * math.log2(math.e)
    cos_t, sin_t = cos.T, sin.T
    tab_qt = c0 * jnp.concatenate(
        [jnp.ones((QK_NOPE, ang.shape[0]), F32), cos_t, cos_t, -sin_t, sin_t], axis=0)
    return tab_k, tab_qt


def kernel(x, mem, positions, norm_mix, w_in, b_conv_in, norm_cq, w_uq, norm_ckv, w_ukv, w_dw, b_dw, ln_conv_g, ln_conv_b, w_pw2, b_pw2, w_o, norm_cross, norm_mem, w_q_mem, w_kv_mem, w_o_mem, norm_ffn, w_gate, w_up, w_down, norm_final):
    batch, seq, d = x.shape
    mem_len = mem.shape[1]
    t = batch * seq
    xf = x.reshape(t, d)
    tab_k, tab_qt = _rope_tables(positions)
    l = 0

    w_all = pack_w_in(w_in[l], tr=256)
    wq3 = w_uq[l].reshape(Q_LORA, MLA_HEADS, QK_NOPE + QK_ROPE)
    w_qt = jnp.concatenate(
        [wq3, _swap_halves(wq3[..., QK_NOPE:])], axis=-1).reshape(Q_LORA, -1).T.astype(BF16)
    wkv3 = w_ukv[l].reshape(KV_LORA, MLA_HEADS, QK_NOPE + V_HEAD)
    w_k = wkv3[..., :QK_NOPE].reshape(KV_LORA, -1).astype(BF16)
    w_vt = wkv3[..., QK_NOPE:].reshape(KV_LORA, -1).T.astype(BF16)

    lat, sg, cv = in_proj_conv(xf, norm_mix[l], w_all, b_conv_in[l], w_dw[l], b_dw[l],
                               ln_conv_g[l], ln_conv_b[l], seq, tm=256)
    qt, k, vt = proj_qkv(lat, norm_cq[l], w_qt, tab_qt, norm_ckv[l], w_k, w_vt, tab_k, tm=512)
    y_a = attention(qt, k, vt, batch, seq)
    x1, h1 = mix(cv, y_a, sg, xf, w_pw2[l].astype(BF16), b_pw2[l], w_o[l].astype(BF16),
                 norm_cross[l], tm=256)

    kvm = mem_kv(mem.reshape(batch * mem_len, d), norm_mem[l], w_kv_mem[l].astype(BF16), tn=1024)
    x2, h2 = cross_attention(h1, x1, kvm, w_q_mem[l].astype(BF16), w_o_mem[l].astype(BF16),
                             norm_ffn[l], batch, seq, mem_len, tm=256)

    a = ffn_up(h2, w_gate[l].astype(BF16), w_up[l].astype(BF16), tm=1024)
    out = ffn_down(a, w_down[l].astype(BF16), x2, norm_final, tm=256)
    return out.reshape(batch, seq, d)
```

```python
import functools
import math

import jax
import jax.numpy as jnp
from jax import lax
from jax.experimental import pallas as pl
from jax.experimental.pallas import tpu as pltpu

F32 = jnp.float32
BF16 = jnp.bfloat16

D_MODEL = 2048
CHUNK = 64
MLA_HEADS = 16
QK_NOPE = 128
QK_ROPE = 64
V_HEAD = 128
Q_LORA = 512
KV_LORA = 256
ROPE_THETA = 10000.0
CONV_CH = 1024
CONV_WIDTH = 31
MEM_HEADS = 4
MEM_HEAD_DIM = D_MODEL // MEM_HEADS
EPS = 1e-6
NEG_INF = -1e30

LANES = 128
SUBLANES = 8
HEAD_PAD = 256
BF16_ROWS = 16
VT_ROWS = V_HEAD + BF16_ROWS
LAT_W = Q_LORA + KV_LORA + 2 * QK_ROPE
HALO = 32
VMEM_LIMIT = 56 * 1024 * 1024

_NT = (((1,), (1,)), ((), ()))


def _cparams(n_axes):
    return pltpu.CompilerParams(
        dimension_semantics=("arbitrary",) * n_axes,
        vmem_limit_bytes=VMEM_LIMIT)


def _sigmoid(x):
    return 1.0 / (1.0 + jnp.exp(-x))


def _rms(x, g):
    ms = jnp.mean(x * x, axis=-1, keepdims=True)
    return x * lax.rsqrt(ms + EPS) * g


def _resident(shape):
    zeros = (0,) * len(shape)
    return pl.BlockSpec(shape, lambda *_: zeros, pipeline_mode=pl.Buffered(1))


W_BLK = 1024
N_WBLK = 7
CONV_GROUPS = 2


def _conv_ln_swish(ext_ref, sh_ref, w_ref, bdw_ref, g_ref, b_ref, o_ref, tm):
    n_sh = tm + HALO - SUBLANES
    for r in range(1, SUBLANES):
        sh_ref[r - 1, :, :] = ext_ref[r:r + n_sh, :]
    base = HALO - (CONV_WIDTH - 1)
    taps = sorted(range(CONV_WIDTH), key=lambda j: ((base + j) % SUBLANES, j))
    for c in range(tm // (SUBLANES * CONV_GROUPS)):
        r0 = c * SUBLANES * CONV_GROUPS
        accs = [bdw_ref[...]] * CONV_GROUPS
        for j in taps:
            phase = (base + j) % SUBLANES
            wj = w_ref[j]
            for gi in range(CONV_GROUPS):
                a0 = r0 + gi * SUBLANES + base + j - phase
                if phase == 0:
                    src = ext_ref[a0:a0 + SUBLANES, :]
                else:
                    src = sh_ref[phase - 1, a0:a0 + SUBLANES, :]
                accs[gi] = accs[gi] + wj * src
        for gi in range(CONV_GROUPS):
            acc = accs[gi]
            mu = jnp.mean(acc, axis=-1, keepdims=True)
            d = acc - mu
            var = jnp.mean(d * d, axis=-1, keepdims=True)
            y = d * lax.rsqrt(var + EPS) * g_ref[...] + b_ref[...]
            rows = slice(r0 + gi * SUBLANES, r0 + (gi + 1) * SUBLANES)
            o_ref[rows, :] = (y * _sigmoid(y)).astype(o_ref.dtype)


def _inproj_kernel(x_ref, g_ref, w0, w1, w2, w3, w4, w5, w6, bu_ref,
                   cw_ref, cb_ref, lg_ref, lb_ref,
                   lat_ref, sg_ref, cv_ref, ext_ref, unew_ref, sh_ref, *, tm, tiles_per_seq):
    i = pl.program_id(0)

    @pl.when(i == 0)
    def _():
        ext_ref[...] = jnp.zeros(ext_ref.shape, F32)

    _conv_ln_swish(ext_ref, sh_ref, cw_ref, cb_ref, lg_ref, lb_ref, cv_ref, tm)

    h = _rms(x_ref[...], g_ref[...]).astype(BF16)
    lat_ref[...] = jnp.dot(h, w0[...], preferred_element_type=F32)
    a = jnp.dot(h, w1[...], preferred_element_type=F32) + bu_ref[:, :CONV_CH]
    gt = jnp.dot(h, w2[...], preferred_element_type=F32) + bu_ref[:, CONV_CH:]
    unew_ref[...] = a * _sigmoid(gt)
    for c, w in enumerate((w3, w4, w5, w6)):
        acc = jnp.dot(h, w[...], preferred_element_type=F32)
        sg_ref[:, c * W_BLK:(c + 1) * W_BLK] = _sigmoid(acc).astype(sg_ref.dtype)

    @pl.when(i % tiles_per_seq == 0)
    def _():
        ext_ref[0:HALO, :] = jnp.zeros((HALO, CONV_CH), F32)

    @pl.when(i % tiles_per_seq != 0)
    def _():
        ext_ref[0:HALO, :] = ext_ref[tm:tm + HALO, :]

    ext_ref[HALO:, :] = unew_ref[...]


def in_proj_conv(x, g, w_all, b_u, w_dw, b_dw, ln_g, ln_b, seq, tm):
    m, d = x.shape
    n = m // tm
    cur = lambda w: pl.BlockSpec((tm, w), lambda i: (jnp.minimum(i, n - 1), 0))
    prev = lambda w: pl.BlockSpec((tm, w), lambda i: (jnp.maximum(i - 1, 0), 0))
    wblk = lambda c: pl.BlockSpec((d, W_BLK), lambda i: (0, c), pipeline_mode=pl.Buffered(1))
    w8 = jnp.broadcast_to(w_dw[:, None, :], (CONV_WIDTH, SUBLANES, CONV_CH))
    vec = lambda a: jnp.broadcast_to(a.reshape(1, CONV_CH), (SUBLANES, CONV_CH))
    return pl.pallas_call(
        functools.partial(_inproj_kernel, tm=tm, tiles_per_seq=seq // tm),
        grid=(n + 1,),
        in_specs=[cur(d), _resident((1, d))] + [wblk(c) for c in range(N_WBLK)]
                 + [_resident((1, 2 * CONV_CH)), _resident((CONV_WIDTH, SUBLANES, CONV_CH)),
                    _resident((SUBLANES, CONV_CH)), _resident((SUBLANES, CONV_CH)),
                    _resident((SUBLANES, CONV_CH))],
        out_specs=[cur(W_BLK), cur(4 * W_BLK), prev(CONV_CH)],
        out_shape=[jax.ShapeDtypeStruct((m, W_BLK), F32),
                   jax.ShapeDtypeStruct((m, 4 * W_BLK), BF16),
                   jax.ShapeDtypeStruct((m, CONV_CH), BF16)],
        scratch_shapes=[pltpu.VMEM((tm + HALO, CONV_CH), F32),
                        pltpu.VMEM((tm, CONV_CH), F32),
                        pltpu.VMEM((SUBLANES - 1, tm + HALO - SUBLANES, CONV_CH), F32)],
        compiler_params=_cparams(1),
        name="in_proj_conv",
    )(x, g.reshape(1, d), *([w_all] * N_WBLK), b_u.reshape(1, 2 * CONV_CH),
      w8, vec(b_dw), vec(ln_g), vec(ln_b))


def _qkv_kernel(cq_ref, gq_ref, wqt_ref, tabq_ref, c_ref, g_ref, kl_ref, tab_ref, wk_ref,
                wvt_ref, qt_ref, k_ref, vt_ref):
    cqn = _rms(cq_ref[...], gq_ref[...]).astype(BF16)
    tabq = tabq_ref[...]
    for hh in range(MLA_HEADS):
        rows = slice(hh * HEAD_PAD, (hh + 1) * HEAD_PAD)
        acc = lax.dot_general(wqt_ref[rows, :], cqn, _NT, preferred_element_type=F32)
        qt_ref[rows, :] = (acc * tabq).astype(qt_ref.dtype)

    cn = _rms(c_ref[...], g_ref[...]).astype(BF16)
    prod = kl_ref[...] * tab_ref[...]
    krot2 = (prod + pltpu.roll(prod, QK_ROPE, 1)).astype(k_ref.dtype)
    for pair in range(MLA_HEADS // 2):
        acc = jnp.dot(cn, wk_ref[:, pair * 256:(pair + 1) * 256],
                      preferred_element_type=F32)
        for sub in range(2):
            hh = 2 * pair + sub
            k_ref[:, hh * HEAD_PAD:hh * HEAD_PAD + QK_NOPE] = (
                acc[:, sub * QK_NOPE:(sub + 1) * QK_NOPE].astype(k_ref.dtype))
            k_ref[:, hh * HEAD_PAD + QK_NOPE:(hh + 1) * HEAD_PAD] = krot2
    vt = lax.dot_general(wvt_ref[...], cn, _NT, preferred_element_type=F32).astype(vt_ref.dtype)
    ones = jnp.ones((BF16_ROWS, vt.shape[1]), vt_ref.dtype)
    for hh in range(MLA_HEADS):
        vt_ref[hh * VT_ROWS:hh * VT_ROWS + V_HEAD, :] = vt[hh * V_HEAD:(hh + 1) * V_HEAD, :]
        vt_ref[hh * VT_ROWS + V_HEAD:(hh + 1) * VT_ROWS, :] = ones


def proj_qkv(lat, g_q, w_qt, tab_qt, g_kv, w_k, w_vt, tab_k, tm):
    m = lat.shape[0]
    ckv_blk = Q_LORA // KV_LORA
    kl_blk = (Q_LORA + KV_LORA) // LANES
    nq, nk, nv = MLA_HEADS * HEAD_PAD, MLA_HEADS * HEAD_PAD, MLA_HEADS * VT_ROWS
    return pl.pallas_call(
        _qkv_kernel,
        grid=(m // tm,),
        in_specs=[pl.BlockSpec((tm, Q_LORA), lambda i: (i, 0)),
                  _resident((1, Q_LORA)),
                  _resident(w_qt.shape),
                  pl.BlockSpec((HEAD_PAD, tm), lambda i: (0, i)),
                  pl.BlockSpec((tm, KV_LORA), lambda i: (i, ckv_blk)),
                  _resident((1, KV_LORA)),
                  pl.BlockSpec((tm, LANES), lambda i: (i, kl_blk)),
                  pl.BlockSpec((tm, LANES), lambda i: (i, 0)),
                  _resident(w_k.shape), _resident(w_vt.shape)],
        out_specs=[pl.BlockSpec((nq, tm), lambda i: (0, i)),
                   pl.BlockSpec((tm, nk), lambda i: (i, 0)),
                   pl.BlockSpec((nv, tm), lambda i: (0, i))],
        out_shape=[jax.ShapeDtypeStruct((nq, m), BF16),
                   jax.ShapeDtypeStruct((m, nk), BF16),
                   jax.ShapeDtypeStruct((nv, m), BF16)],
        compiler_params=_cparams(1),
        name="proj_qkv",
    )(lat, g_q.reshape(1, Q_LORA), w_qt, tab_qt,
      lat, g_kv.reshape(1, KV_LORA), lat, tab_k, w_k, w_vt)


ATT_TQ = 256
ATT_HEADS = 2
ATT_LOOKAHEAD = 4


def _attn_kernel(qt_ref, k_ref, vt_ref, o_ref, *, seq):
    tq = ATT_TQ
    g_diag = tq // SUBLANES
    key_chunk = lax.broadcasted_iota(jnp.int32, (g_diag, SUBLANES, tq), 0) // (CHUNK // SUBLANES)
    qry_chunk = lax.broadcasted_iota(jnp.int32, (g_diag, SUBLANES, tq), 2) // CHUNK
    diag_mask = key_chunk <= qry_chunk

    def scores(hh, qi):
        kend = (qi + 1) * tq
        s = jnp.dot(k_ref[0:kend, hh * HEAD_PAD:(hh + 1) * HEAD_PAD],
                    qt_ref[hh * HEAD_PAD:(hh + 1) * HEAD_PAD, qi * tq:kend],
                    preferred_element_type=F32)
        return s.reshape(kend // SUBLANES, SUBLANES, tq)

    stages = [(hh, qi) for qi in range(seq // tq) for hh in range(ATT_HEADS)]
    pending = [scores(*st) for st in stages[:ATT_LOOKAHEAD]]
    for idx, (hh, qi) in enumerate(stages):
        q0, kend = qi * tq, (qi + 1) * tq
        s = pending.pop(0)
        if idx + ATT_LOOKAHEAD < len(stages):
            pending.append(scores(*stages[idx + ATT_LOOKAHEAD]))
        g_past = q0 // SUBLANES
        s_d = jnp.where(diag_mask, s[g_past:], NEG_INF)
        m8 = jnp.max(s_d, axis=0)
        if qi > 0:
            m8 = jnp.maximum(m8, jnp.max(s[:g_past], axis=0))
        m8 = jnp.broadcast_to(jnp.max(m8, axis=0, keepdims=True), (SUBLANES, tq))
        p = jnp.exp2(s_d - m8[None])
        if qi > 0:
            p = jnp.concatenate([jnp.exp2(s[:g_past] - m8[None]), p], axis=0)
        p = p.reshape(kend, tq).astype(BF16)
        ot = jnp.dot(vt_ref[hh * VT_ROWS:(hh + 1) * VT_ROWS, 0:kend], p,
                     preferred_element_type=F32)
        inv = 1.0 / ot[V_HEAD:V_HEAD + 1, :]
        o_ref[q0:kend, hh * V_HEAD:(hh + 1) * V_HEAD] = (
            (ot[:V_HEAD, :] * inv).T.astype(o_ref.dtype))


def attention(qt, k, vt, batch, seq):
    nh = ATT_HEADS
    return pl.pallas_call(
        functools.partial(_attn_kernel, seq=seq),
        grid=(batch, MLA_HEADS // nh),
        in_specs=[pl.BlockSpec((nh * HEAD_PAD, seq), lambda b, h: (h, b)),
                  pl.BlockSpec((seq, nh * HEAD_PAD), lambda b, h: (b, h)),
                  pl.BlockSpec((nh * VT_ROWS, seq), lambda b, h: (h, b))],
        out_specs=pl.BlockSpec((seq, nh * V_HEAD), lambda b, h: (b, h)),
        out_shape=jax.ShapeDtypeStruct((batch * seq, MLA_HEADS * V_HEAD), BF16),
        compiler_params=_cparams(2),
        name="mla_attention",
    )(qt, k, vt)


def _mix_kernel(v_ref, ya_ref, sga_ref, sgb_ref, x_ref, wpw_ref, bpw_ref, wo_ref, g_ref,
                x1_ref, h1_ref):
    yb = jnp.dot(v_ref[...], wpw_ref[...], preferred_element_type=F32) + bpw_ref[...]
    merged = (sga_ref[...].astype(F32) * ya_ref[...].astype(F32)
              + sgb_ref[...].astype(F32) * yb)
    x1 = x_ref[...] + jnp.dot(merged.astype(BF16), wo_ref[...], preferred_element_type=F32)
    x1_ref[...] = x1
    h1_ref[...] = _rms(x1, g_ref[...]).astype(h1_ref.dtype)


def mix(v, ya, sg, x, w_pw2, b_pw2, w_o, g, tm):
    m, d = x.shape
    row = lambda w: pl.BlockSpec((tm, w), lambda i: (i, 0))
    return pl.pallas_call(
        _mix_kernel,
        grid=(m // tm,),
        in_specs=[row(CONV_CH), row(d),
                  pl.BlockSpec((tm, d), lambda i: (i, 0)),
                  pl.BlockSpec((tm, d), lambda i: (i, 1)),
                  row(d),
                  _resident((CONV_CH, d)), _resident((1, d)), _resident((d, d)),
                  _resident((1, d))],
        out_specs=[row(d), row(d)],
        out_shape=[jax.ShapeDtypeStruct((m, d), F32), jax.ShapeDtypeStruct((m, d), BF16)],
        compiler_params=_cparams(1),
        name="mix_out_proj",
    )(v, ya, sg, sg, x, w_pw2, b_pw2.reshape(1, d), w_o, g.reshape(1, d))


def _memkv_kernel(m_ref, g_ref, w_ref, o_ref, mn_ref):
    @pl.when(pl.program_id(0) == 0)
    def _():
        mn_ref[...] = _rms(m_ref[...], g_ref[...]).astype(mn_ref.dtype)

    o_ref[...] = jnp.dot(mn_ref[...], w_ref[...], preferred_element_type=F32).astype(o_ref.dtype)


def mem_kv(mem, g, w, tn):
    m, k = mem.shape
    n = w.shape[1]
    return pl.pallas_call(
        _memkv_kernel,
        grid=(n // tn,),
        in_specs=[_resident((m, k)), _resident((1, k)),
                  pl.BlockSpec((k, tn), lambda j: (0, j))],
        out_specs=pl.BlockSpec((m, tn), lambda j: (0, j)),
        out_shape=jax.ShapeDtypeStruct((m, n), BF16),
        scratch_shapes=[pltpu.VMEM((m, k), BF16)],
        compiler_params=_cparams(1),
        name="mem_kv",
    )(mem, g.reshape(1, k), w)


def _cross_kernel(h_ref, x_ref, kv_ref, wq_ref, wo_ref, g_ref, x2_ref, h2_ref, o_scr):
    qm = jnp.dot(h_ref[...], wq_ref[...], preferred_element_type=F32)
    scale = MEM_HEAD_DIM ** -0.5

    def scores(hd):
        sl = slice(hd * MEM_HEAD_DIM, (hd + 1) * MEM_HEAD_DIM)
        return lax.dot_general(qm[:, sl].astype(BF16), kv_ref[:, sl], _NT,
                               preferred_element_type=F32) * scale

    nxt = scores(0)
    for hd in range(MEM_HEADS):
        sl = slice(hd * MEM_HEAD_DIM, (hd + 1) * MEM_HEAD_DIM)
        vsl = slice(D_MODEL + hd * MEM_HEAD_DIM, D_MODEL + (hd + 1) * MEM_HEAD_DIM)
        s = nxt
        if hd + 1 < MEM_HEADS:
            nxt = scores(hd + 1)
        m = jnp.max(s, axis=-1, keepdims=True)
        p = jnp.exp(s - m)
        l = jnp.sum(p, axis=-1, keepdims=True)
        o = jnp.dot(p.astype(BF16), kv_ref[:, vsl], preferred_element_type=F32)
        o_scr[:, sl] = (o * (1.0 / l)).astype(o_scr.dtype)
    x2 = x_ref[...] + jnp.dot(o_scr[...], wo_ref[...], preferred_element_type=F32)
    x2_ref[...] = x2
    h2_ref[...] = _rms(x2, g_ref[...]).astype(h2_ref.dtype)


def cross_attention(h1, x1, kvm, w_q, w_o, g, batch, seq, mem_len, tm):
    d = D_MODEL
    nt = seq // tm
    row = pl.BlockSpec((tm, d), lambda b, s: (b * nt + s, 0))
    return pl.pallas_call(
        _cross_kernel,
        grid=(batch, nt),
        in_specs=[row, row,
                  pl.BlockSpec((mem_len, 2 * d), lambda b, s: (b, 0)),
                  _resident((d, d)), _resident((d, d)), _resident((1, d))],
        out_specs=[row, row],
        out_shape=[jax.ShapeDtypeStruct((batch * seq, d), F32),
                   jax.ShapeDtypeStruct((batch * seq, d), BF16)],
        scratch_shapes=[pltpu.VMEM((tm, d), BF16)],
        compiler_params=_cparams(2),
        name="cross_attention",
    )(h1, x1, kvm, w_q, w_o, g.reshape(1, d))


FFN_TN = 512


def _ffn_up_kernel(h_ref, wg_ref, wu_ref, o_ref):
    h = h_ref[...]
    gt = jnp.dot(h, wg_ref[...], preferred_element_type=F32)
    up = jnp.dot(h, wu_ref[...], preferred_element_type=F32)
    o_ref[...] = (gt * _sigmoid(gt) * up).astype(o_ref.dtype)


def ffn_up(h, w_gate, w_up, tm):
    m, k = h.shape
    n = w_gate.shape[1]
    wspec = pl.BlockSpec((k, FFN_TN), lambda j, i: (0, j))
    return pl.pallas_call(
        _ffn_up_kernel,
        grid=(n // FFN_TN, m // tm),
        in_specs=[pl.BlockSpec((tm, k), lambda j, i: (i, 0)), wspec, wspec],
        out_specs=pl.BlockSpec((tm, FFN_TN), lambda j, i: (i, j)),
        out_shape=jax.ShapeDtypeStruct((m, n), BF16),
        compiler_params=_cparams(2),
        name="ffn_up",
    )(h, w_gate, w_up)


def _ffn_down_kernel(a_ref, w_ref, x_ref, g_ref, o_ref):
    x3 = x_ref[...] + jnp.dot(a_ref[...], w_ref[...], preferred_element_type=F32)
    o_ref[...] = _rms(x3, g_ref[...])


def ffn_down(a, w, x, g, tm):
    m, kdim = a.shape
    d = w.shape[1]
    return pl.pallas_call(
        _ffn_down_kernel,
        grid=(m // tm,),
        in_specs=[pl.BlockSpec((tm, kdim), lambda i: (i, 0)),
                  _resident((kdim, d)),
                  pl.BlockSpec((tm, d), lambda i: (i, 0)),
                  _resident((1, d))],
        out_specs=pl.BlockSpec((tm, d), lambda i: (i, 0)),
        out_shape=jax.ShapeDtypeStruct((m, d), F32),
        compiler_params=_cparams(1),
        name="ffn_down",
    )(a, w, x, g.reshape(1, d))


def _swap_halves(w):
    half = w.shape[-1] // 2
    return jnp.concatenate([w[..., half:], w[..., :half]], axis=-1)


PACK_TAIL = 256


def _pack_w_kernel(tail_ref, blk_ref, o_ref):
    j = pl.program_id(0)
    o_kr = Q_LORA + KV_LORA
    half = QK_ROPE // 2
    lat_src = o_kr + QK_ROPE

    @pl.when(j == 0)
    def _():
        b = blk_ref[...]
        rows = jnp.concatenate(
            [b[0:lat_src], b[o_kr + half:lat_src], b[o_kr:o_kr + half],
             jnp.zeros((W_BLK - LAT_W, b.shape[1]), b.dtype)], axis=0)
        o_ref[...] = rows.T.astype(o_ref.dtype)

    @pl.when(j > 0)
    def _():
        n_tail = W_BLK - lat_src
        rows = jnp.concatenate(
            [tail_ref[PACK_TAIL - n_tail:PACK_TAIL, :], blk_ref[0:lat_src, :]], axis=0)
        o_ref[...] = rows.T.astype(o_ref.dtype)


def pack_w_in(w_t):
    n_in, d = w_t.shape
    tails_per_blk = W_BLK // PACK_TAIL
    return pl.pallas_call(
        _pack_w_kernel,
        grid=(N_WBLK,),
        in_specs=[pl.BlockSpec((PACK_TAIL, d),
                               lambda j: (jnp.maximum(j, 1) * tails_per_blk - 1, 0)),
                  pl.BlockSpec((W_BLK, d), lambda j: (j, 0))],
        out_specs=pl.BlockSpec((d, W_BLK), lambda j: (0, j)),
        out_shape=jax.ShapeDtypeStruct((d, N_WBLK * W_BLK), BF16),
        compiler_params=_cparams(1),
        name="pack_w_in",
    )(w_t, w_t)


def _rope_tables(positions):
    inv_freq = 1.0 / (ROPE_THETA ** (jnp.arange(0, QK_ROPE, 2, dtype=F32) / QK_ROPE))
    ang = positions.astype(F32).reshape(-1, 1) * inv_freq
    cos, sin = jnp.cos(ang), jnp.sin(ang)
    tab_k = jnp.concatenate([cos, cos, -sin, sin], axis=-1)
    c0 = (QK_NOPE + QK_ROPE) ** -0.5 * math.log2(math.e)
    cos_t, sin_t = cos.T, sin.T
    tab_qt = c0 * jnp.concatenate(
        [jnp.ones((QK_NOPE, ang.shape[0]), F32), cos_t, cos_t, -sin_t, sin_t], axis=0)
    return tab_k, tab_qt


def kernel(x, mem, positions, norm_mix, w_in, b_conv_in, norm_cq, w_uq, norm_ckv, w_ukv, w_dw, b_dw, ln_conv_g, ln_conv_b, w_pw2, b_pw2, w_o, norm_cross, norm_mem, w_q_mem, w_kv_mem, w_o_mem, norm_ffn, w_gate, w_up, w_down, norm_final):
    batch, seq, d = x.shape
    mem_len = mem.shape[1]
    t = batch * seq
    xf = x.reshape(t, d)
    tab_k, tab_qt = _rope_tables(positions)
    l = 0

    w_all = pack_w_in(w_in[l].T)
    wq3 = w_uq[l].reshape(Q_LORA, MLA_HEADS, QK_NOPE + QK_ROPE)
    w_qt = jnp.concatenate(
        [wq3, _swap_halves(wq3[..., QK_NOPE:])], axis=-1).reshape(Q_LORA, -1).T.astype(BF16)
    wkv3 = w_ukv[l].reshape(KV_LORA, MLA_HEADS, QK_NOPE + V_HEAD)
    w_k = wkv3[..., :QK_NOPE].reshape(KV_LORA, -1).astype(BF16)
    w_vt = wkv3[..., QK_NOPE:].reshape(KV_LORA, -1).T.astype(BF16)

    lat, sg, cv = in_proj_conv(xf, norm_mix[l], w_all, b_conv_in[l], w_dw[l], b_dw[l],
                               ln_conv_g[l], ln_conv_b[l], seq, tm=256)
    qt, k, vt = proj_qkv(lat, norm_cq[l], w_qt, tab_qt, norm_ckv[l], w_k, w_vt, tab_k, tm=512)
    y_a = attention(qt, k, vt, batch, seq)
    x1, h1 = mix(cv, y_a, sg, xf, w_pw2[l].astype(BF16), b_pw2[l], w_o[l].astype(BF16),
                 norm_cross[l], tm=256)

    kvm = mem_kv(mem.reshape(batch * mem_len, d), norm_mem[l], w_kv_mem[l].astype(BF16), tn=1024)
    x2, h2 = cross_attention(h1, x1, kvm, w_q_mem[l].astype(BF16), w_o_mem[l].astype(BF16),
                             norm_ffn[l], batch, seq, mem_len, tm=256)

    a = ffn_up(h2, w_gate[l].astype(BF16), w_up[l].astype(BF16), tm=1024)
    out = ffn_down(a, w_down[l].astype(BF16), x2, norm_final, tm=256)
    return out.reshape(batch, seq, d)
```

```python
import functools
import math

import jax
import jax.numpy as jnp
from jax import lax
from jax.experimental import pallas as pl
from jax.experimental.pallas import tpu as pltpu

F32 = jnp.float32
BF16 = jnp.bfloat16

D_MODEL = 2048
CHUNK = 64
MLA_HEADS = 16
QK_NOPE = 128
QK_ROPE = 64
V_HEAD = 128
Q_LORA = 512
KV_LORA = 256
ROPE_THETA = 10000.0
CONV_CH = 1024
CONV_WIDTH = 31
MEM_HEADS = 4
MEM_HEAD_DIM = D_MODEL // MEM_HEADS
EPS = 1e-6
NEG_INF = -1e30

LANES = 128
SUBLANES = 8
HEAD_PAD = 256
BF16_ROWS = 16
VT_ROWS = V_HEAD + BF16_ROWS
LAT_W = Q_LORA + KV_LORA + 2 * QK_ROPE
HALO = 32
VMEM_LIMIT = 56 * 1024 * 1024

_NT = (((1,), (1,)), ((), ()))


def _cparams(n_axes):
    return pltpu.CompilerParams(
        dimension_semantics=("arbitrary",) * n_axes,
        vmem_limit_bytes=VMEM_LIMIT)


def _sigmoid(x):
    return 1.0 / (1.0 + jnp.exp(-x))


def _rms(x, g):
    ms = jnp.mean(x * x, axis=-1, keepdims=True)
    return x * lax.rsqrt(ms + EPS) * g


def _resident(shape):
    zeros = (0,) * len(shape)
    return pl.BlockSpec(shape, lambda *_: zeros, pipeline_mode=pl.Buffered(1))


W_BLK = 1024
N_WBLK = 7
CONV_GROUPS = 2


def _conv_ln_swish(ext_ref, sh_ref, w_ref, bdw_ref, g_ref, b_ref, o_ref, tm):
    n_sh = tm + HALO - SUBLANES
    for r in range(1, SUBLANES):
        sh_ref[r - 1, :, :] = ext_ref[r:r + n_sh, :]
    base = HALO - (CONV_WIDTH - 1)
    taps = sorted(range(CONV_WIDTH), key=lambda j: ((base + j) % SUBLANES, j))
    for c in range(tm // (SUBLANES * CONV_GROUPS)):
        r0 = c * SUBLANES * CONV_GROUPS
        accs = [bdw_ref[...]] * CONV_GROUPS
        for j in taps:
            phase = (base + j) % SUBLANES
            wj = w_ref[j]
            for gi in range(CONV_GROUPS):
                a0 = r0 + gi * SUBLANES + base + j - phase
                if phase == 0:
                    src = ext_ref[a0:a0 + SUBLANES, :]
                else:
                    src = sh_ref[phase - 1, a0:a0 + SUBLANES, :]
                accs[gi] = accs[gi] + wj * src
        for gi in range(CONV_GROUPS):
            acc = accs[gi]
            mu = jnp.mean(acc, axis=-1, keepdims=True)
            d = acc - mu
            var = jnp.mean(d * d, axis=-1, keepdims=True)
            y = d * lax.rsqrt(var + EPS) * g_ref[...] + b_ref[...]
            rows = slice(r0 + gi * SUBLANES, r0 + (gi + 1) * SUBLANES)
            o_ref[rows, :] = (y * _sigmoid(y)).astype(o_ref.dtype)


def _inproj_kernel(*refs, tm, tiles_per_seq, n_cast):
    (x_ref, g_ref, w0, w1, w2, w3, w4, w5, w6, bu_ref, cw_ref, cb_ref, lg_ref, lb_ref) = refs[:14]
    cast_in = refs[14:14 + n_cast]
    lat_ref, sg_ref, cv_ref = refs[14 + n_cast:17 + n_cast]
    cast_out = refs[17 + n_cast:17 + 2 * n_cast]
    ext_ref, unew_ref, sh_ref = refs[17 + 2 * n_cast:]
    i = pl.program_id(0)

    @pl.when(i == 0)
    def _():
        ext_ref[...] = jnp.zeros(ext_ref.shape, F32)

    for src, dst in zip(cast_in, cast_out):
        dst[...] = src[...].astype(dst.dtype)

    _conv_ln_swish(ext_ref, sh_ref, cw_ref, cb_ref, lg_ref, lb_ref, cv_ref, tm)

    h = _rms(x_ref[...], g_ref[...]).astype(BF16)
    lat_ref[...] = jnp.dot(h, w0[...], preferred_element_type=F32)
    a = jnp.dot(h, w1[...], preferred_element_type=F32) + bu_ref[:, :CONV_CH]
    gt = jnp.dot(h, w2[...], preferred_element_type=F32) + bu_ref[:, CONV_CH:]
    unew_ref[...] = a * _sigmoid(gt)
    for c, w in enumerate((w3, w4, w5, w6)):
        acc = jnp.dot(h, w[...], preferred_element_type=F32)
        sg_ref[:, c * W_BLK:(c + 1) * W_BLK] = _sigmoid(acc).astype(sg_ref.dtype)

    @pl.when(i % tiles_per_seq == 0)
    def _():
        ext_ref[0:HALO, :] = jnp.zeros((HALO, CONV_CH), F32)

    @pl.when(i % tiles_per_seq != 0)
    def _():
        ext_ref[0:HALO, :] = ext_ref[tm:tm + HALO, :]

    ext_ref[HALO:, :] = unew_ref[...]


def in_proj_conv(x, g, w_all, b_u, w_dw, b_dw, ln_g, ln_b, cast_weights, seq, tm):
    m, d = x.shape
    n = m // tm
    cur = lambda w: pl.BlockSpec((tm, w), lambda i: (jnp.minimum(i, n - 1), 0))
    prev = lambda w: pl.BlockSpec((tm, w), lambda i: (jnp.maximum(i - 1, 0), 0))
    wblk = lambda c: pl.BlockSpec((d, W_BLK), lambda i: (0, c), pipeline_mode=pl.Buffered(1))
    cast_specs = [pl.BlockSpec((w.shape[0] // n, w.shape[1]), lambda i: (jnp.minimum(i, n - 1), 0))
                  for w in cast_weights]
    w8 = jnp.broadcast_to(w_dw[:, None, :], (CONV_WIDTH, SUBLANES, CONV_CH))
    vec = lambda a: jnp.broadcast_to(a.reshape(1, CONV_CH), (SUBLANES, CONV_CH))
    outs = pl.pallas_call(
        functools.partial(_inproj_kernel, tm=tm, tiles_per_seq=seq // tm,
                          n_cast=len(cast_weights)),
        grid=(n + 1,),
        in_specs=[cur(d), _resident((1, d))] + [wblk(c) for c in range(N_WBLK)]
                 + [_resident((1, 2 * CONV_CH)), _resident((CONV_WIDTH, SUBLANES, CONV_CH)),
                    _resident((SUBLANES, CONV_CH)), _resident((SUBLANES, CONV_CH)),
                    _resident((SUBLANES, CONV_CH))] + cast_specs,
        out_specs=[cur(W_BLK), cur(4 * W_BLK), prev(CONV_CH)] + cast_specs,
        out_shape=[jax.ShapeDtypeStruct((m, W_BLK), F32),
                   jax.ShapeDtypeStruct((m, 4 * W_BLK), BF16),
                   jax.ShapeDtypeStruct((m, CONV_CH), BF16)]
                  + [jax.ShapeDtypeStruct(w.shape, BF16) for w in cast_weights],
        scratch_shapes=[pltpu.VMEM((tm + HALO, CONV_CH), F32),
                        pltpu.VMEM((tm, CONV_CH), F32),
                        pltpu.VMEM((SUBLANES - 1, tm + HALO - SUBLANES, CONV_CH), F32)],
        compiler_params=_cparams(1),
        name="in_proj_conv",
    )(x, g.reshape(1, d), *([w_all] * N_WBLK), b_u.reshape(1, 2 * CONV_CH),
      w8, vec(b_dw), vec(ln_g), vec(ln_b), *cast_weights)
    return outs[0], outs[1], outs[2], outs[3:]


def _qkv_kernel(cq_ref, gq_ref, wqt_ref, tabq_ref, c_ref, g_ref, kl_ref, tab_ref, wk_ref,
                wvt_ref, qt_ref, k_ref, vt_ref):
    cqn = _rms(cq_ref[...], gq_ref[...]).astype(BF16)
    tabq = tabq_ref[...]
    for hh in range(MLA_HEADS):
        rows = slice(hh * HEAD_PAD, (hh + 1) * HEAD_PAD)
        acc = lax.dot_general(wqt_ref[rows, :], cqn, _NT, preferred_element_type=F32)
        qt_ref[rows, :] = (acc * tabq).astype(qt_ref.dtype)

    cn = _rms(c_ref[...], g_ref[...]).astype(BF16)
    prod = kl_ref[...] * tab_ref[...]
    krot2 = (prod + pltpu.roll(prod, QK_ROPE, 1)).astype(k_ref.dtype)
    for pair in range(MLA_HEADS // 2):
        acc = jnp.dot(cn, wk_ref[:, pair * 256:(pair + 1) * 256],
                      preferred_element_type=F32)
        for sub in range(2):
            hh = 2 * pair + sub
            k_ref[:, hh * HEAD_PAD:hh * HEAD_PAD + QK_NOPE] = (
                acc[:, sub * QK_NOPE:(sub + 1) * QK_NOPE].astype(k_ref.dtype))
            k_ref[:, hh * HEAD_PAD + QK_NOPE:(hh + 1) * HEAD_PAD] = krot2
    vt = lax.dot_general(wvt_ref[...], cn, _NT, preferred_element_type=F32).astype(vt_ref.dtype)
    ones = jnp.ones((BF16_ROWS, vt.shape[1]), vt_ref.dtype)
    for hh in range(MLA_HEADS):
        vt_ref[hh * VT_ROWS:hh * VT_ROWS + V_HEAD, :] = vt[hh * V_HEAD:(hh + 1) * V_HEAD, :]
        vt_ref[hh * VT_ROWS + V_HEAD:(hh + 1) * VT_ROWS, :] = ones


def proj_qkv(lat, g_q, w_qt, tab_qt, g_kv, w_k, w_vt, tab_k, tm):
    m = lat.shape[0]
    ckv_blk = Q_LORA // KV_LORA
    kl_blk = (Q_LORA + KV_LORA) // LANES
    nq, nk, nv = MLA_HEADS * HEAD_PAD, MLA_HEADS * HEAD_PAD, MLA_HEADS * VT_ROWS
    return pl.pallas_call(
        _qkv_kernel,
        grid=(m // tm,),
        in_specs=[pl.BlockSpec((tm, Q_LORA), lambda i: (i, 0)),
                  _resident((1, Q_LORA)),
                  _resident(w_qt.shape),
                  pl.BlockSpec((HEAD_PAD, tm), lambda i: (0, i)),
                  pl.BlockSpec((tm, KV_LORA), lambda i: (i, ckv_blk)),
                  _resident((1, KV_LORA)),
                  pl.BlockSpec((tm, LANES), lambda i: (i, kl_blk)),
                  pl.BlockSpec((tm, LANES), lambda i: (i, 0)),
                  _resident(w_k.shape), _resident(w_vt.shape)],
        out_specs=[pl.BlockSpec((nq, tm), lambda i: (0, i)),
                   pl.BlockSpec((tm, nk), lambda i: (i, 0)),
                   pl.BlockSpec((nv, tm), lambda i: (0, i))],
        out_shape=[jax.ShapeDtypeStruct((nq, m), BF16),
                   jax.ShapeDtypeStruct((m, nk), BF16),
                   jax.ShapeDtypeStruct((nv, m), BF16)],
        compiler_params=_cparams(1),
        name="proj_qkv",
    )(lat, g_q.reshape(1, Q_LORA), w_qt, tab_qt,
      lat, g_kv.reshape(1, KV_LORA), lat, tab_k, w_k, w_vt)


ATT_TQ = 256
ATT_HEADS = 2
ATT_LOOKAHEAD = 4


def _attn_kernel(qt_ref, k_ref, vt_ref, o_ref, *, seq):
    tq = ATT_TQ
    g_diag = tq // SUBLANES
    key_chunk = lax.broadcasted_iota(jnp.int32, (g_diag, SUBLANES, tq), 0) // (CHUNK // SUBLANES)
    qry_chunk = lax.broadcasted_iota(jnp.int32, (g_diag, SUBLANES, tq), 2) // CHUNK
    diag_mask = key_chunk <= qry_chunk

    def scores(hh, qi):
        kend = (qi + 1) * tq
        s = jnp.dot(k_ref[0:kend, hh * HEAD_PAD:(hh + 1) * HEAD_PAD],
                    qt_ref[hh * HEAD_PAD:(hh + 1) * HEAD_PAD, qi * tq:kend],
                    preferred_element_type=F32)
        return s.reshape(kend // SUBLANES, SUBLANES, tq)

    stages = [(hh, qi) for qi in range(seq // tq) for hh in range(ATT_HEADS)]
    pending = [scores(*st) for st in stages[:ATT_LOOKAHEAD]]
    for idx, (hh, qi) in enumerate(stages):
        q0, kend = qi * tq, (qi + 1) * tq
        s = pending.pop(0)
        if idx + ATT_LOOKAHEAD < len(stages):
            pending.append(scores(*stages[idx + ATT_LOOKAHEAD]))
        g_past = q0 // SUBLANES
        s_d = jnp.where(diag_mask, s[g_past:], NEG_INF)
        m8 = jnp.max(s_d, axis=0)
        if qi > 0:
            m8 = jnp.maximum(m8, jnp.max(s[:g_past], axis=0))
        m8 = jnp.broadcast_to(jnp.max(m8, axis=0, keepdims=True), (SUBLANES, tq))
        p = jnp.exp2(s_d - m8[None])
        if qi > 0:
            p = jnp.concatenate([jnp.exp2(s[:g_past] - m8[None]), p], axis=0)
        p = p.reshape(kend, tq).astype(BF16)
        ot = jnp.dot(vt_ref[hh * VT_ROWS:(hh + 1) * VT_ROWS, 0:kend], p,
                     preferred_element_type=F32)
        inv = 1.0 / ot[V_HEAD:V_HEAD + 1, :]
        o_ref[q0:kend, hh * V_HEAD:(hh + 1) * V_HEAD] = (
            (ot[:V_HEAD, :] * inv).T.astype(o_ref.dtype))


def attention(qt, k, vt, batch, seq):
    nh = ATT_HEADS
    return pl.pallas_call(
        functools.partial(_attn_kernel, seq=seq),
        grid=(batch, MLA_HEADS // nh),
        in_specs=[pl.BlockSpec((nh * HEAD_PAD, seq), lambda b, h: (h, b)),
                  pl.BlockSpec((seq, nh * HEAD_PAD), lambda b, h: (b, h)),
                  pl.BlockSpec((nh * VT_ROWS, seq), lambda b, h: (h, b))],
        out_specs=pl.BlockSpec((seq, nh * V_HEAD), lambda b, h: (b, h)),
        out_shape=jax.ShapeDtypeStruct((batch * seq, MLA_HEADS * V_HEAD), BF16),
        compiler_params=_cparams(2),
        name="mla_attention",
    )(qt, k, vt)


def _mix_kernel(v_ref, ya_ref, sga_ref, sgb_ref, x_ref, wpw_ref, bpw_ref, wo_ref, g_ref,
                x1_ref, h1_ref):
    yb = jnp.dot(v_ref[...], wpw_ref[...], preferred_element_type=F32) + bpw_ref[...]
    merged = (sga_ref[...].astype(F32) * ya_ref[...].astype(F32)
              + sgb_ref[...].astype(F32) * yb)
    x1 = x_ref[...] + jnp.dot(merged.astype(BF16), wo_ref[...], preferred_element_type=F32)
    x1_ref[...] = x1
    h1_ref[...] = _rms(x1, g_ref[...]).astype(h1_ref.dtype)


def mix(v, ya, sg, x, w_pw2, b_pw2, w_o, g, tm):
    m, d = x.shape
    row = lambda w: pl.BlockSpec((tm, w), lambda i: (i, 0))
    return pl.pallas_call(
        _mix_kernel,
        grid=(m // tm,),
        in_specs=[row(CONV_CH), row(d),
                  pl.BlockSpec((tm, d), lambda i: (i, 0)),
                  pl.BlockSpec((tm, d), lambda i: (i, 1)),
                  row(d),
                  _resident((CONV_CH, d)), _resident((1, d)), _resident((d, d)),
                  _resident((1, d))],
        out_specs=[row(d), row(d)],
        out_shape=[jax.ShapeDtypeStruct((m, d), F32), jax.ShapeDtypeStruct((m, d), BF16)],
        compiler_params=_cparams(1),
        name="mix_out_proj",
    )(v, ya, sg, sg, x, w_pw2, b_pw2.reshape(1, d), w_o, g.reshape(1, d))


def _memkv_kernel(m_ref, g_ref, w_ref, o_ref, mn_ref):
    @pl.when(pl.program_id(0) == 0)
    def _():
        mn_ref[...] = _rms(m_ref[...], g_ref[...]).astype(mn_ref.dtype)

    o_ref[...] = jnp.dot(mn_ref[...], w_ref[...], preferred_element_type=F32).astype(o_ref.dtype)


def mem_kv(mem, g, w, tn):
    m, k = mem.shape
    n = w.shape[1]
    return pl.pallas_call(
        _memkv_kernel,
        grid=(n // tn,),
        in_specs=[_resident((m, k)), _resident((1, k)),
                  pl.BlockSpec((k, tn), lambda j: (0, j))],
        out_specs=pl.BlockSpec((m, tn), lambda j: (0, j)),
        out_shape=jax.ShapeDtypeStruct((m, n), BF16),
        scratch_shapes=[pltpu.VMEM((m, k), BF16)],
        compiler_params=_cparams(1),
        name="mem_kv",
    )(mem, g.reshape(1, k), w)


def _cross_kernel(h_ref, x_ref, kv_ref, wq_ref, wo_ref, g_ref, x2_ref, h2_ref, o_scr):
    qm = jnp.dot(h_ref[...], wq_ref[...], preferred_element_type=F32)
    scale = MEM_HEAD_DIM ** -0.5

    def scores(hd):
        sl = slice(hd * MEM_HEAD_DIM, (hd + 1) * MEM_HEAD_DIM)
        return lax.dot_general(qm[:, sl].astype(BF16), kv_ref[:, sl], _NT,
                               preferred_element_type=F32) * scale

    nxt = scores(0)
    for hd in range(MEM_HEADS):
        sl = slice(hd * MEM_HEAD_DIM, (hd + 1) * MEM_HEAD_DIM)
        vsl = slice(D_MODEL + hd * MEM_HEAD_DIM, D_MODEL + (hd + 1) * MEM_HEAD_DIM)
        s = nxt
        if hd + 1 < MEM_HEADS:
            nxt = scores(hd + 1)
        m = jnp.max(s, axis=-1, keepdims=True)
        p = jnp.exp(s - m)
        l = jnp.sum(p, axis=-1, keepdims=True)
        o = jnp.dot(p.astype(BF16), kv_ref[:, vsl], preferred_element_type=F32)
        o_scr[:, sl] = (o * (1.0 / l)).astype(o_scr.dtype)
    x2 = x_ref[...] + jnp.dot(o_scr[...], wo_ref[...], preferred_element_type=F32)
    x2_ref[...] = x2
    h2_ref[...] = _rms(x2, g_ref[...]).astype(h2_ref.dtype)


def cross_attention(h1, x1, kvm, w_q, w_o, g, batch, seq, mem_len, tm):
    d = D_MODEL
    nt = seq // tm
    row = pl.BlockSpec((tm, d), lambda b, s: (b * nt + s, 0))
    return pl.pallas_call(
        _cross_kernel,
        grid=(batch, nt),
        in_specs=[row, row,
                  pl.BlockSpec((mem_len, 2 * d), lambda b, s: (b, 0)),
                  _resident((d, d)), _resident((d, d)), _resident((1, d))],
        out_specs=[row, row],
        out_shape=[jax.ShapeDtypeStruct((batch * seq, d), F32),
                   jax.ShapeDtypeStruct((batch * seq, d), BF16)],
        scratch_shapes=[pltpu.VMEM((tm, d), BF16)],
        compiler_params=_cparams(2),
        name="cross_attention",
    )(h1, x1, kvm, w_q, w_o, g.reshape(1, d))


FFN_TN = 512


def _ffn_up_kernel(h_ref, wg_ref, wu_ref, wd_ref, o_ref, wd_bf_ref, wg_s, wu_s):
    @pl.when(pl.program_id(1) == 0)
    def _():
        wg_s[...] = wg_ref[...].astype(BF16)
        wu_s[...] = wu_ref[...].astype(BF16)
        wd_bf_ref[...] = wd_ref[...].astype(BF16)

    h = h_ref[...]
    gt = jnp.dot(h, wg_s[...], preferred_element_type=F32)
    up = jnp.dot(h, wu_s[...], preferred_element_type=F32)
    o_ref[...] = (gt * _sigmoid(gt) * up).astype(o_ref.dtype)


def ffn_up(h, w_gate, w_up, w_down, tm):
    m, k = h.shape
    n = w_gate.shape[1]
    d = w_down.shape[1]
    wspec = pl.BlockSpec((k, FFN_TN), lambda j, i: (0, j))
    wdspec = pl.BlockSpec((FFN_TN, d), lambda j, i: (j, 0))
    return pl.pallas_call(
        _ffn_up_kernel,
        grid=(n // FFN_TN, m // tm),
        in_specs=[pl.BlockSpec((tm, k), lambda j, i: (i, 0)), wspec, wspec, wdspec],
        out_specs=[pl.BlockSpec((tm, FFN_TN), lambda j, i: (i, j)), wdspec],
        out_shape=[jax.ShapeDtypeStruct((m, n), BF16), jax.ShapeDtypeStruct((n, d), BF16)],
        scratch_shapes=[pltpu.VMEM((k, FFN_TN), BF16), pltpu.VMEM((k, FFN_TN), BF16)],
        compiler_params=_cparams(2),
        name="ffn_up",
    )(h, w_gate, w_up, w_down)


def _ffn_down_kernel(a_ref, w_ref, x_ref, g_ref, o_ref):
    x3 = x_ref[...] + jnp.dot(a_ref[...], w_ref[...], preferred_element_type=F32)
    o_ref[...] = _rms(x3, g_ref[...])


def ffn_down(a, w, x, g, tm):
    m, kdim = a.shape
    d = w.shape[1]
    return pl.pallas_call(
        _ffn_down_kernel,
        grid=(m // tm,),
        in_specs=[pl.BlockSpec((tm, kdim), lambda i: (i, 0)),
                  _resident((kdim, d)),
                  pl.BlockSpec((tm, d), lambda i: (i, 0)),
                  _resident((1, d))],
        out_specs=pl.BlockSpec((tm, d), lambda i: (i, 0)),
        out_shape=jax.ShapeDtypeStruct((m, d), F32),
        compiler_params=_cparams(1),
        name="ffn_down",
    )(a, w, x, g.reshape(1, d))


def _swap_halves(w):
    half = w.shape[-1] // 2
    return jnp.concatenate([w[..., half:], w[..., :half]], axis=-1)


PACK_TAIL = 256


def _pack_w_kernel(tail_ref, blk_ref, o_ref):
    j = pl.program_id(0)
    o_kr = Q_LORA + KV_LORA
    half = QK_ROPE // 2
    lat_src = o_kr + QK_ROPE

    @pl.when(j == 0)
    def _():
        b = blk_ref[...]
        rows = jnp.concatenate(
            [b[0:lat_src], b[o_kr + half:lat_src], b[o_kr:o_kr + half],
             jnp.zeros((W_BLK - LAT_W, b.shape[1]), b.dtype)], axis=0)
        o_ref[...] = rows.T.astype(o_ref.dtype)

    @pl.when(j > 0)
    def _():
        n_tail = W_BLK - lat_src
        rows = jnp.concatenate(
            [tail_ref[PACK_TAIL - n_tail:PACK_TAIL, :], blk_ref[0:lat_src, :]], axis=0)
        o_ref[...] = rows.T.astype(o_ref.dtype)


def pack_w_in(w_t):
    n_in, d = w_t.shape
    tails_per_blk = W_BLK // PACK_TAIL
    return pl.pallas_call(
        _pack_w_kernel,
        grid=(N_WBLK,),
        in_specs=[pl.BlockSpec((PACK_TAIL, d),
                               lambda j: (jnp.maximum(j, 1) * tails_per_blk - 1, 0)),
                  pl.BlockSpec((W_BLK, d), lambda j: (j, 0))],
        out_specs=pl.BlockSpec((d, W_BLK), lambda j: (0, j)),
        out_shape=jax.ShapeDtypeStruct((d, N_WBLK * W_BLK), BF16),
        compiler_params=_cparams(1),
        name="pack_w_in",
    )(w_t, w_t)


def _rope_tables(positions):
    inv_freq = 1.0 / (ROPE_THETA ** (jnp.arange(0, QK_ROPE, 2, dtype=F32) / QK_ROPE))
    ang = positions.astype(F32).reshape(-1, 1) * inv_freq
    cos, sin = jnp.cos(ang), jnp.sin(ang)
    tab_k = jnp.concatenate([cos, cos, -sin, sin], axis=-1)
    c0 = (QK_NOPE + QK_ROPE) ** -0.5 * math.log2(math.e)
    cos_t, sin_t = cos.T, sin.T
    tab_qt = c0 * jnp.concatenate(
        [jnp.ones((QK_NOPE, ang.shape[0]), F32), cos_t, cos_t, -sin_t, sin_t], axis=0)
    return tab_k, tab_qt


def kernel(x, mem, positions, norm_mix, w_in, b_conv_in, norm_cq, w_uq, norm_ckv, w_ukv, w_dw, b_dw, ln_conv_g, ln_conv_b, w_pw2, b_pw2, w_o, norm_cross, norm_mem, w_q_mem, w_kv_mem, w_o_mem, norm_ffn, w_gate, w_up, w_down, norm_final):
    batch, seq, d = x.shape
    mem_len = mem.shape[1]
    t = batch * seq
    xf = x.reshape(t, d)
    tab_k, tab_qt = _rope_tables(positions)
    l = 0

    w_all = pack_w_in(w_in[l].T)
    wq3 = w_uq[l].reshape(Q_LORA, MLA_HEADS, QK_NOPE + QK_ROPE)
    w_qt = jnp.concatenate(
        [wq3, _swap_halves(wq3[..., QK_NOPE:])], axis=-1).reshape(Q_LORA, -1).T.astype(BF16)
    wkv3 = w_ukv[l].reshape(KV_LORA, MLA_HEADS, QK_NOPE + V_HEAD)
    w_k = wkv3[..., :QK_NOPE].reshape(KV_LORA, -1).astype(BF16)
    w_vt = wkv3[..., QK_NOPE:].reshape(KV_LORA, -1).T.astype(BF16)

    later_weights = (w_pw2[l], w_o[l], w_q_mem[l], w_o_mem[l], w_kv_mem[l])
    lat, sg, cv, (w_pw2_b, w_o_b, w_qm_b, w_om_b, w_kvm_b) = in_proj_conv(
        xf, norm_mix[l], w_all, b_conv_in[l], w_dw[l], b_dw[l], ln_conv_g[l], ln_conv_b[l],
        later_weights, seq, tm=256)
    qt, k, vt = proj_qkv(lat, norm_cq[l], w_qt, tab_qt, norm_ckv[l], w_k, w_vt, tab_k, tm=512)
    y_a = attention(qt, k, vt, batch, seq)
    x1, h1 = mix(cv, y_a, sg, xf, w_pw2_b, b_pw2[l], w_o_b, norm_cross[l], tm=256)

    kvm = mem_kv(mem.reshape(batch * mem_len, d), norm_mem[l], w_kvm_b, tn=1024)
    x2, h2 = cross_attention(h1, x1, kvm, w_qm_b, w_om_b, norm_ffn[l], batch, seq, mem_len,
                             tm=256)

    a, w_down_b = ffn_up(h2, w_gate[l], w_up[l], w_down[l], tm=1024)
    out = ffn_down(a, w_down_b, x2, norm_final, tm=256)
    return out.reshape(batch, seq, d)
```

```python
import functools
import math

import jax
import jax.numpy as jnp
from jax import lax
from jax.experimental import pallas as pl
from jax.experimental.pallas import tpu as pltpu

F32 = jnp.float32
BF16 = jnp.bfloat16

D_MODEL = 2048
CHUNK = 64
MLA_HEADS = 16
QK_NOPE = 128
QK_ROPE = 64
V_HEAD = 128
Q_LORA = 512
KV_LORA = 256
ROPE_THETA = 10000.0
CONV_CH = 1024
CONV_WIDTH = 31
MEM_HEADS = 4
MEM_HEAD_DIM = D_MODEL // MEM_HEADS
EPS = 1e-6
NEG_INF = -1e30

LANES = 128
SUBLANES = 8
HEAD_PAD = 256
BF16_ROWS = 16
VT_ROWS = V_HEAD + BF16_ROWS
LAT_W = Q_LORA + KV_LORA + 2 * QK_ROPE
HALO = 32
VMEM_LIMIT = 56 * 1024 * 1024

_NT = (((1,), (1,)), ((), ()))


def _cparams(n_axes):
    return pltpu.CompilerParams(
        dimension_semantics=("arbitrary",) * n_axes,
        vmem_limit_bytes=VMEM_LIMIT)


def _sigmoid(x):
    return 1.0 / (1.0 + jnp.exp(-x))


def _rms(x, g):
    ms = jnp.mean(x * x, axis=-1, keepdims=True)
    return x * lax.rsqrt(ms + EPS) * g


def _resident(shape):
    zeros = (0,) * len(shape)
    return pl.BlockSpec(shape, lambda *_: zeros, pipeline_mode=pl.Buffered(1))


W_BLK = 1024
N_WBLK = 7
CONV_GROUPS = 2


def _conv_ln_swish(ext_ref, sh_ref, w_ref, bdw_ref, g_ref, b_ref, o_ref, tm):
    n_sh = tm + HALO - SUBLANES
    for r in range(1, SUBLANES):
        sh_ref[r - 1, :, :] = ext_ref[r:r + n_sh, :]
    base = HALO - (CONV_WIDTH - 1)
    taps = sorted(range(CONV_WIDTH), key=lambda j: ((base + j) % SUBLANES, j))
    for c in range(tm // (SUBLANES * CONV_GROUPS)):
        r0 = c * SUBLANES * CONV_GROUPS
        accs = [bdw_ref[...]] * CONV_GROUPS
        for j in taps:
            phase = (base + j) % SUBLANES
            wj = w_ref[j]
            for gi in range(CONV_GROUPS):
                a0 = r0 + gi * SUBLANES + base + j - phase
                if phase == 0:
                    src = ext_ref[a0:a0 + SUBLANES, :]
                else:
                    src = sh_ref[phase - 1, a0:a0 + SUBLANES, :]
                accs[gi] = accs[gi] + wj * src
        for gi in range(CONV_GROUPS):
            acc = accs[gi]
            mu = jnp.mean(acc, axis=-1, keepdims=True)
            d = acc - mu
            var = jnp.mean(d * d, axis=-1, keepdims=True)
            y = d * lax.rsqrt(var + EPS) * g_ref[...] + b_ref[...]
            rows = slice(r0 + gi * SUBLANES, r0 + (gi + 1) * SUBLANES)
            o_ref[rows, :] = (y * _sigmoid(y)).astype(o_ref.dtype)


def _inproj_kernel(*refs, tm, tiles_per_seq, n_cast):
    (x_ref, g_ref, w0, w1, w2, w3, w4, w5, w6, bu_ref, cw_ref, cb_ref, lg_ref, lb_ref) = refs[:14]
    cast_in = refs[14:14 + n_cast]
    lat_ref, sg_ref, cv_ref = refs[14 + n_cast:17 + n_cast]
    cast_out = refs[17 + n_cast:17 + 2 * n_cast]
    ext_ref, unew_ref, sh_ref = refs[17 + 2 * n_cast:]
    i = pl.program_id(0)

    @pl.when(i == 0)
    def _():
        ext_ref[...] = jnp.zeros(ext_ref.shape, F32)

    for src, dst in zip(cast_in, cast_out):
        dst[...] = src[...].astype(dst.dtype)

    _conv_ln_swish(ext_ref, sh_ref, cw_ref, cb_ref, lg_ref, lb_ref, cv_ref, tm)

    h = _rms(x_ref[...], g_ref[...]).astype(BF16)
    lat_ref[...] = jnp.dot(h, w0[...], preferred_element_type=F32)
    a = jnp.dot(h, w1[...], preferred_element_type=F32) + bu_ref[:, :CONV_CH]
    gt = jnp.dot(h, w2[...], preferred_element_type=F32) + bu_ref[:, CONV_CH:]
    unew_ref[...] = a * _sigmoid(gt)
    for c, w in enumerate((w3, w4, w5, w6)):
        acc = jnp.dot(h, w[...], preferred_element_type=F32)
        sg_ref[:, c * W_BLK:(c + 1) * W_BLK] = _sigmoid(acc).astype(sg_ref.dtype)

    @pl.when(i % tiles_per_seq == 0)
    def _():
        ext_ref[0:HALO, :] = jnp.zeros((HALO, CONV_CH), F32)

    @pl.when(i % tiles_per_seq != 0)
    def _():
        ext_ref[0:HALO, :] = ext_ref[tm:tm + HALO, :]

    ext_ref[HALO:, :] = unew_ref[...]


def in_proj_conv(x, g, w_all, b_u, w_dw, b_dw, ln_g, ln_b, cast_weights, seq, tm):
    m, d = x.shape
    n = m // tm
    cur = lambda w: pl.BlockSpec((tm, w), lambda i: (jnp.minimum(i, n - 1), 0))
    prev = lambda w: pl.BlockSpec((tm, w), lambda i: (jnp.maximum(i - 1, 0), 0))
    wblk = lambda c: pl.BlockSpec((d, W_BLK), lambda i: (0, c), pipeline_mode=pl.Buffered(1))
    cast_specs = [pl.BlockSpec((w.shape[0] // n, w.shape[1]), lambda i: (jnp.minimum(i, n - 1), 0))
                  for w in cast_weights]
    w8 = jnp.broadcast_to(w_dw[:, None, :], (CONV_WIDTH, SUBLANES, CONV_CH))
    vec = lambda a: jnp.broadcast_to(a.reshape(1, CONV_CH), (SUBLANES, CONV_CH))
    outs = pl.pallas_call(
        functools.partial(_inproj_kernel, tm=tm, tiles_per_seq=seq // tm,
                          n_cast=len(cast_weights)),
        grid=(n + 1,),
        in_specs=[cur(d), _resident((1, d))] + [wblk(c) for c in range(N_WBLK)]
                 + [_resident((1, 2 * CONV_CH)), _resident((CONV_WIDTH, SUBLANES, CONV_CH)),
                    _resident((SUBLANES, CONV_CH)), _resident((SUBLANES, CONV_CH)),
                    _resident((SUBLANES, CONV_CH))] + cast_specs,
        out_specs=[cur(W_BLK), cur(4 * W_BLK), prev(CONV_CH)] + cast_specs,
        out_shape=[jax.ShapeDtypeStruct((m, W_BLK), F32),
                   jax.ShapeDtypeStruct((m, 4 * W_BLK), BF16),
                   jax.ShapeDtypeStruct((m, CONV_CH), BF16)]
                  + [jax.ShapeDtypeStruct(w.shape, BF16) for w in cast_weights],
        scratch_shapes=[pltpu.VMEM((tm + HALO, CONV_CH), F32),
                        pltpu.VMEM((tm, CONV_CH), F32),
                        pltpu.VMEM((SUBLANES - 1, tm + HALO - SUBLANES, CONV_CH), F32)],
        compiler_params=_cparams(1),
        name="in_proj_conv",
    )(x, g.reshape(1, d), *([w_all] * N_WBLK), b_u.reshape(1, 2 * CONV_CH),
      w8, vec(b_dw), vec(ln_g), vec(ln_b), *cast_weights)
    return outs[0], outs[1], outs[2], outs[3:]


def _qkv_kernel(cq_ref, gq_ref, wqt_ref, c_ref, g_ref, kl_ref, tab_ref, wk_ref,
                wvt_ref, qt_ref, k_ref, vt_ref):
    cqn = _rms(cq_ref[...], gq_ref[...]).astype(BF16)
    c0 = (QK_NOPE + QK_ROPE) ** -0.5 * math.log2(math.e)
    tab_t = tab_ref[...].T * c0
    for hh in range(MLA_HEADS):
        r0 = hh * HEAD_PAD
        acc = lax.dot_general(wqt_ref[r0:r0 + HEAD_PAD, :], cqn, _NT,
                              preferred_element_type=F32)
        qt_ref[r0:r0 + QK_NOPE, :] = (acc[:QK_NOPE] * c0).astype(qt_ref.dtype)
        qt_ref[r0 + QK_NOPE:r0 + HEAD_PAD, :] = (acc[QK_NOPE:] * tab_t).astype(qt_ref.dtype)

    cn = _rms(c_ref[...], g_ref[...]).astype(BF16)
    prod = kl_ref[...] * tab_ref[...]
    krot2 = (prod + pltpu.roll(prod, QK_ROPE, 1)).astype(k_ref.dtype)
    for pair in range(MLA_HEADS // 2):
        acc = jnp.dot(cn, wk_ref[:, pair * 256:(pair + 1) * 256],
                      preferred_element_type=F32)
        for sub in range(2):
            hh = 2 * pair + sub
            k_ref[:, hh * HEAD_PAD:hh * HEAD_PAD + QK_NOPE] = (
                acc[:, sub * QK_NOPE:(sub + 1) * QK_NOPE].astype(k_ref.dtype))
            k_ref[:, hh * HEAD_PAD + QK_NOPE:(hh + 1) * HEAD_PAD] = krot2
    vt = lax.dot_general(wvt_ref[...], cn, _NT, preferred_element_type=F32).astype(vt_ref.dtype)
    ones = jnp.ones((BF16_ROWS, vt.shape[1]), vt_ref.dtype)
    for hh in range(MLA_HEADS):
        vt_ref[hh * VT_ROWS:hh * VT_ROWS + V_HEAD, :] = vt[hh * V_HEAD:(hh + 1) * V_HEAD, :]
        vt_ref[hh * VT_ROWS + V_HEAD:(hh + 1) * VT_ROWS, :] = ones


def proj_qkv(lat, g_q, w_qt, g_kv, w_k, w_vt, tab_k, tm):
    m = lat.shape[0]
    ckv_blk = Q_LORA // KV_LORA
    kl_blk = (Q_LORA + KV_LORA) // LANES
    nq, nk, nv = MLA_HEADS * HEAD_PAD, MLA_HEADS * HEAD_PAD, MLA_HEADS * VT_ROWS
    return pl.pallas_call(
        _qkv_kernel,
        grid=(m // tm,),
        in_specs=[pl.BlockSpec((tm, Q_LORA), lambda i: (i, 0)),
                  _resident((1, Q_LORA)),
                  _resident(w_qt.shape),
                  pl.BlockSpec((tm, KV_LORA), lambda i: (i, ckv_blk)),
                  _resident((1, KV_LORA)),
                  pl.BlockSpec((tm, LANES), lambda i: (i, kl_blk)),
                  pl.BlockSpec((tm, LANES), lambda i: (i, 0)),
                  _resident(w_k.shape), _resident(w_vt.shape)],
        out_specs=[pl.BlockSpec((nq, tm), lambda i: (0, i)),
                   pl.BlockSpec((tm, nk), lambda i: (i, 0)),
                   pl.BlockSpec((nv, tm), lambda i: (0, i))],
        out_shape=[jax.ShapeDtypeStruct((nq, m), BF16),
                   jax.ShapeDtypeStruct((m, nk), BF16),
                   jax.ShapeDtypeStruct((nv, m), BF16)],
        compiler_params=_cparams(1),
        name="proj_qkv",
    )(lat, g_q.reshape(1, Q_LORA), w_qt,
      lat, g_kv.reshape(1, KV_LORA), lat, tab_k, w_k, w_vt)


ATT_TQ = 256
ATT_HEADS = 2
ATT_LOOKAHEAD = 4


def _attn_kernel(qt_ref, k_ref, vt_ref, o_ref, *, seq):
    tq = ATT_TQ
    g_diag = tq // SUBLANES
    key_chunk = lax.broadcasted_iota(jnp.int32, (g_diag, SUBLANES, tq), 0) // (CHUNK // SUBLANES)
    qry_chunk = lax.broadcasted_iota(jnp.int32, (g_diag, SUBLANES, tq), 2) // CHUNK
    diag_mask = key_chunk <= qry_chunk

    def scores(hh, qi):
        kend = (qi + 1) * tq
        s = jnp.dot(k_ref[0:kend, hh * HEAD_PAD:(hh + 1) * HEAD_PAD],
                    qt_ref[hh * HEAD_PAD:(hh + 1) * HEAD_PAD, qi * tq:kend],
                    preferred_element_type=F32)
        return s.reshape(kend // SUBLANES, SUBLANES, tq)

    stages = [(hh, qi) for qi in range(seq // tq) for hh in range(ATT_HEADS)]
    pending = [scores(*st) for st in stages[:ATT_LOOKAHEAD]]
    for idx, (hh, qi) in enumerate(stages):
        q0, kend = qi * tq, (qi + 1) * tq
        s = pending.pop(0)
        if idx + ATT_LOOKAHEAD < len(stages):
            pending.append(scores(*stages[idx + ATT_LOOKAHEAD]))
        g_past = q0 // SUBLANES
        s_d = jnp.where(diag_mask, s[g_past:], NEG_INF)
        m8 = jnp.max(s_d, axis=0)
        if qi > 0:
            m8 = jnp.maximum(m8, jnp.max(s[:g_past], axis=0))
        m8 = jnp.broadcast_to(jnp.max(m8, axis=0, keepdims=True), (SUBLANES, tq))
        p = jnp.exp2(s_d - m8[None])
        if qi > 0:
            p = jnp.concatenate([jnp.exp2(s[:g_past] - m8[None]), p], axis=0)
        p = p.reshape(kend, tq).astype(BF16)
        ot = jnp.dot(vt_ref[hh * VT_ROWS:(hh + 1) * VT_ROWS, 0:kend], p,
                     preferred_element_type=F32)
        inv = 1.0 / ot[V_HEAD:V_HEAD + 1, :]
        o_ref[q0:kend, hh * V_HEAD:(hh + 1) * V_HEAD] = (
            (ot[:V_HEAD, :] * inv).T.astype(o_ref.dtype))


def attention(qt, k, vt, batch, seq):
    nh = ATT_HEADS
    return pl.pallas_call(
        functools.partial(_attn_kernel, seq=seq),
        grid=(batch, MLA_HEADS // nh),
        in_specs=[pl.BlockSpec((nh * HEAD_PAD, seq), lambda b, h: (h, b)),
                  pl.BlockSpec((seq, nh * HEAD_PAD), lambda b, h: (b, h)),
                  pl.BlockSpec((nh * VT_ROWS, seq), lambda b, h: (h, b))],
        out_specs=pl.BlockSpec((seq, nh * V_HEAD), lambda b, h: (b, h)),
        out_shape=jax.ShapeDtypeStruct((batch * seq, MLA_HEADS * V_HEAD), BF16),
        compiler_params=_cparams(2),
        name="mla_attention",
    )(qt, k, vt)


def _mix_kernel(v_ref, ya_ref, sga_ref, sgb_ref, x_ref, wpw_ref, bpw_ref, wo_ref, g_ref,
                x1_ref, h1_ref):
    yb = jnp.dot(v_ref[...], wpw_ref[...], preferred_element_type=F32) + bpw_ref[...]
    merged = (sga_ref[...].astype(F32) * ya_ref[...].astype(F32)
              + sgb_ref[...].astype(F32) * yb)
    x1 = x_ref[...] + jnp.dot(merged.astype(BF16), wo_ref[...], preferred_element_type=F32)
    x1_ref[...] = x1
    h1_ref[...] = _rms(x1, g_ref[...]).astype(h1_ref.dtype)


def mix(v, ya, sg, x, w_pw2, b_pw2, w_o, g, tm):
    m, d = x.shape
    row = lambda w: pl.BlockSpec((tm, w), lambda i: (i, 0))
    return pl.pallas_call(
        _mix_kernel,
        grid=(m // tm,),
        in_specs=[row(CONV_CH), row(d),
                  pl.BlockSpec((tm, d), lambda i: (i, 0)),
                  pl.BlockSpec((tm, d), lambda i: (i, 1)),
                  row(d),
                  _resident((CONV_CH, d)), _resident((1, d)), _resident((d, d)),
                  _resident((1, d))],
        out_specs=[row(d), row(d)],
        out_shape=[jax.ShapeDtypeStruct((m, d), F32), jax.ShapeDtypeStruct((m, d), BF16)],
        compiler_params=_cparams(1),
        name="mix_out_proj",
    )(v, ya, sg, sg, x, w_pw2, b_pw2.reshape(1, d), w_o, g.reshape(1, d))


def _memkv_kernel(m_ref, g_ref, w_ref, o_ref, mn_ref):
    @pl.when(pl.program_id(0) == 0)
    def _():
        mn_ref[...] = _rms(m_ref[...], g_ref[...]).astype(mn_ref.dtype)

    o_ref[...] = jnp.dot(mn_ref[...], w_ref[...], preferred_element_type=F32).astype(o_ref.dtype)


def mem_kv(mem, g, w, tn):
    m, k = mem.shape
    n = w.shape[1]
    return pl.pallas_call(
        _memkv_kernel,
        grid=(n // tn,),
        in_specs=[_resident((m, k)), _resident((1, k)),
                  pl.BlockSpec((k, tn), lambda j: (0, j))],
        out_specs=pl.BlockSpec((m, tn), lambda j: (0, j)),
        out_shape=jax.ShapeDtypeStruct((m, n), BF16),
        scratch_shapes=[pltpu.VMEM((m, k), BF16)],
        compiler_params=_cparams(1),
        name="mem_kv",
    )(mem, g.reshape(1, k), w)


def _cross_kernel(*refs, n_cast):
    h_ref, x_ref, kv_ref, wq_ref, wo_ref, g_ref = refs[:6]
    cast_in = refs[6:6 + n_cast]
    x2_ref, h2_ref = refs[6 + n_cast:8 + n_cast]
    cast_out = refs[8 + n_cast:8 + 2 * n_cast]
    o_scr = refs[8 + 2 * n_cast]

    for src, dst in zip(cast_in, cast_out):
        dst[...] = src[...].astype(dst.dtype)

    qm = jnp.dot(h_ref[...], wq_ref[...], preferred_element_type=F32)
    scale = MEM_HEAD_DIM ** -0.5

    def scores(hd):
        sl = slice(hd * MEM_HEAD_DIM, (hd + 1) * MEM_HEAD_DIM)
        return lax.dot_general(qm[:, sl].astype(BF16), kv_ref[:, sl], _NT,
                               preferred_element_type=F32) * scale

    nxt = scores(0)
    for hd in range(MEM_HEADS):
        sl = slice(hd * MEM_HEAD_DIM, (hd + 1) * MEM_HEAD_DIM)
        vsl = slice(D_MODEL + hd * MEM_HEAD_DIM, D_MODEL + (hd + 1) * MEM_HEAD_DIM)
        s = nxt
        if hd + 1 < MEM_HEADS:
            nxt = scores(hd + 1)
        m = jnp.max(s, axis=-1, keepdims=True)
        p = jnp.exp(s - m)
        l = jnp.sum(p, axis=-1, keepdims=True)
        o = jnp.dot(p.astype(BF16), kv_ref[:, vsl], preferred_element_type=F32)
        o_scr[:, sl] = (o * (1.0 / l)).astype(o_scr.dtype)
    x2 = x_ref[...] + jnp.dot(o_scr[...], wo_ref[...], preferred_element_type=F32)
    x2_ref[...] = x2
    h2_ref[...] = _rms(x2, g_ref[...]).astype(h2_ref.dtype)


def _cast_slice_rows(total_rows, n_steps):
    rows = -(-total_rows // n_steps)
    while total_rows % rows or rows % BF16_ROWS:
        rows += 1
    return rows


def cross_attention(h1, x1, kvm, w_q, w_o, g, cast_weights, batch, seq, mem_len, tm):
    d = D_MODEL
    nt = seq // tm
    row = pl.BlockSpec((tm, d), lambda b, s: (b * nt + s, 0))

    def cast_spec(w):
        rows = _cast_slice_rows(w.shape[0], batch * nt)
        last = w.shape[0] // rows - 1
        return pl.BlockSpec((rows, w.shape[1]), lambda b, s: (jnp.minimum(b * nt + s, last), 0))

    cast_specs = [cast_spec(w) for w in cast_weights]
    outs = pl.pallas_call(
        functools.partial(_cross_kernel, n_cast=len(cast_weights)),
        grid=(batch, nt),
        in_specs=[row, row,
                  pl.BlockSpec((mem_len, 2 * d), lambda b, s: (b, 0)),
                  _resident((d, d)), _resident((d, d)), _resident((1, d))] + cast_specs,
        out_specs=[row, row] + cast_specs,
        out_shape=[jax.ShapeDtypeStruct((batch * seq, d), F32),
                   jax.ShapeDtypeStruct((batch * seq, d), BF16)]
                  + [jax.ShapeDtypeStruct(w.shape, BF16) for w in cast_weights],
        scratch_shapes=[pltpu.VMEM((tm, d), BF16)],
        compiler_params=_cparams(2),
        name="cross_attention",
    )(h1, x1, kvm, w_q, w_o, g.reshape(1, d), *cast_weights)
    return outs[0], outs[1], outs[2:]


FFN_TN = 512


def _ffn_up_kernel(h_ref, wg_ref, wu_ref, o_ref):
    h = h_ref[...]
    gt = jnp.dot(h, wg_ref[...], preferred_element_type=F32)
    up = jnp.dot(h, wu_ref[...], preferred_element_type=F32)
    o_ref[...] = (gt * _sigmoid(gt) * up).astype(o_ref.dtype)


def ffn_up(h, w_gate, w_up, tm):
    m, k = h.shape
    n = w_gate.shape[1]
    wspec = pl.BlockSpec((k, FFN_TN), lambda j, i: (0, j))
    return pl.pallas_call(
        _ffn_up_kernel,
        grid=(n // FFN_TN, m // tm),
        in_specs=[pl.BlockSpec((tm, k), lambda j, i: (i, 0)), wspec, wspec],
        out_specs=pl.BlockSpec((tm, FFN_TN), lambda j, i: (i, j)),
        out_shape=jax.ShapeDtypeStruct((m, n), BF16),
        compiler_params=_cparams(2),
        name="ffn_up",
    )(h, w_gate, w_up)


def _ffn_down_kernel(a_ref, w_ref, x_ref, g_ref, o_ref):
    x3 = x_ref[...] + jnp.dot(a_ref[...], w_ref[...], preferred_element_type=F32)
    o_ref[...] = _rms(x3, g_ref[...])


def ffn_down(a, w, x, g, tm):
    m, kdim = a.shape
    d = w.shape[1]
    return pl.pallas_call(
        _ffn_down_kernel,
        grid=(m // tm,),
        in_specs=[pl.BlockSpec((tm, kdim), lambda i: (i, 0)),
                  _resident((kdim, d)),
                  pl.BlockSpec((tm, d), lambda i: (i, 0)),
                  _resident((1, d))],
        out_specs=pl.BlockSpec((tm, d), lambda i: (i, 0)),
        out_shape=jax.ShapeDtypeStruct((m, d), F32),
        compiler_params=_cparams(1),
        name="ffn_down",
    )(a, w, x, g.reshape(1, d))


def _swap_halves(w):
    half = w.shape[-1] // 2
    return jnp.concatenate([w[..., half:], w[..., :half]], axis=-1)


PACK_TAIL = 256


def _pack_w_kernel(tail_ref, blk_ref, o_ref):
    j = pl.program_id(0)
    o_kr = Q_LORA + KV_LORA
    half = QK_ROPE // 2
    lat_src = o_kr + QK_ROPE

    @pl.when(j == 0)
    def _():
        b = blk_ref[...]
        rows = jnp.concatenate(
            [b[0:lat_src], b[o_kr + half:lat_src], b[o_kr:o_kr + half],
             jnp.zeros((W_BLK - LAT_W, b.shape[1]), b.dtype)], axis=0)
        o_ref[...] = rows.T.astype(o_ref.dtype)

    @pl.when(j > 0)
    def _():
        n_tail = W_BLK - lat_src
        rows = jnp.concatenate(
            [tail_ref[PACK_TAIL - n_tail:PACK_TAIL, :], blk_ref[0:lat_src, :]], axis=0)
        o_ref[...] = rows.T.astype(o_ref.dtype)


def pack_w_in(w_t):
    n_in, d = w_t.shape
    tails_per_blk = W_BLK // PACK_TAIL
    return pl.pallas_call(
        _pack_w_kernel,
        grid=(N_WBLK,),
        in_specs=[pl.BlockSpec((PACK_TAIL, d),
                               lambda j: (jnp.maximum(j, 1) * tails_per_blk - 1, 0)),
                  pl.BlockSpec((W_BLK, d), lambda j: (j, 0))],
        out_specs=pl.BlockSpec((d, W_BLK), lambda j: (0, j)),
        out_shape=jax.ShapeDtypeStruct((d, N_WBLK * W_BLK), BF16),
        compiler_params=_cparams(1),
        name="pack_w_in",
    )(w_t, w_t)


def _rope_table(positions):
    inv_freq = 1.0 / (ROPE_THETA ** (jnp.arange(0, QK_ROPE, 2, dtype=F32) / QK_ROPE))
    ang = positions.astype(F32).reshape(-1, 1) * inv_freq
    cos, sin = jnp.cos(ang), jnp.sin(ang)
    return jnp.concatenate([cos, cos, -sin, sin], axis=-1)


def kernel(x, mem, positions, norm_mix, w_in, b_conv_in, norm_cq, w_uq, norm_ckv, w_ukv, w_dw, b_dw, ln_conv_g, ln_conv_b, w_pw2, b_pw2, w_o, norm_cross, norm_mem, w_q_mem, w_kv_mem, w_o_mem, norm_ffn, w_gate, w_up, w_down, norm_final):
    batch, seq, d = x.shape
    mem_len = mem.shape[1]
    t = batch * seq
    xf = x.reshape(t, d)
    tab_k = _rope_table(positions)
    l = 0

    w_all = pack_w_in(w_in[l].T)
    wq3 = w_uq[l].reshape(Q_LORA, MLA_HEADS, QK_NOPE + QK_ROPE)
    w_qt = jnp.concatenate(
        [wq3, _swap_halves(wq3[..., QK_NOPE:])], axis=-1).reshape(Q_LORA, -1).T.astype(BF16)
    wkv3 = w_ukv[l].reshape(KV_LORA, MLA_HEADS, QK_NOPE + V_HEAD)
    w_k = wkv3[..., :QK_NOPE].reshape(KV_LORA, -1).astype(BF16)
    w_vt = wkv3[..., QK_NOPE:].reshape(KV_LORA, -1).T.astype(BF16)

    later_weights = (w_pw2[l], w_o[l], w_q_mem[l], w_o_mem[l], w_kv_mem[l])
    lat, sg, cv, (w_pw2_b, w_o_b, w_qm_b, w_om_b, w_kvm_b) = in_proj_conv(
        xf, norm_mix[l], w_all, b_conv_in[l], w_dw[l], b_dw[l], ln_conv_g[l], ln_conv_b[l],
        later_weights, seq, tm=256)
    qt, k, vt = proj_qkv(lat, norm_cq[l], w_qt, norm_ckv[l], w_k, w_vt, tab_k, tm=512)
    y_a = attention(qt, k, vt, batch, seq)
    x1, h1 = mix(cv, y_a, sg, xf, w_pw2_b, b_pw2[l], w_o_b, norm_cross[l], tm=256)

    kvm = mem_kv(mem.reshape(batch * mem_len, d), norm_mem[l], w_kvm_b, tn=1024)
    x2, h2, (w_gate_b, w_up_b, w_down_b) = cross_attention(
        h1, x1, kvm, w_qm_b, w_om_b, norm_ffn[l], (w_gate[l], w_up[l], w_down[l]),
        batch, seq, mem_len, tm=256)

    a = ffn_up(h2, w_gate_b, w_up_b, tm=1024)
    out = ffn_down(a, w_down_b, x2, norm_final, tm=256)
    return out.reshape(batch, seq, d)
```

```python
import functools
import math

import jax
import jax.numpy as jnp
from jax import lax
from jax.experimental import pallas as pl
from jax.experimental.pallas import tpu as pltpu

F32 = jnp.float32
BF16 = jnp.bfloat16

D_MODEL = 2048
CHUNK = 64
MLA_HEADS = 16
QK_NOPE = 128
QK_ROPE = 64
V_HEAD = 128
Q_LORA = 512
KV_LORA = 256
ROPE_THETA = 10000.0
CONV_CH = 1024
CONV_WIDTH = 31
MEM_HEADS = 4
MEM_HEAD_DIM = D_MODEL // MEM_HEADS
EPS = 1e-6
NEG_INF = -1e30

LANES = 128
SUBLANES = 8
HEAD_PAD = 256
BF16_ROWS = 16
VT_ROWS = V_HEAD + BF16_ROWS
LAT_W = Q_LORA + KV_LORA + 2 * QK_ROPE
HALO = 32
VMEM_LIMIT = 56 * 1024 * 1024

_NT = (((1,), (1,)), ((), ()))


def _cparams(n_axes):
    return pltpu.CompilerParams(
        dimension_semantics=("arbitrary",) * n_axes,
        vmem_limit_bytes=VMEM_LIMIT)


def _sigmoid(x):
    return 1.0 / (1.0 + jnp.exp(-x))


def _rms(x, g):
    ms = jnp.mean(x * x, axis=-1, keepdims=True)
    return x * lax.rsqrt(ms + EPS) * g


def _resident(shape):
    zeros = (0,) * len(shape)
    return pl.BlockSpec(shape, lambda *_: zeros, pipeline_mode=pl.Buffered(1))


W_BLK = 1024
N_WBLK = 7
CONV_GROUPS = 2


def _conv_ln_swish(ext_ref, sh_ref, w_ref, bdw_ref, g_ref, b_ref, o_ref, tm):
    n_sh = tm + HALO - SUBLANES
    for r in range(1, SUBLANES):
        sh_ref[r - 1, :, :] = ext_ref[r:r + n_sh, :]
    base = HALO - (CONV_WIDTH - 1)
    taps = sorted(range(CONV_WIDTH), key=lambda j: ((base + j) % SUBLANES, j))
    for c in range(tm // (SUBLANES * CONV_GROUPS)):
        r0 = c * SUBLANES * CONV_GROUPS
        accs = [bdw_ref[...]] * CONV_GROUPS
        for j in taps:
            phase = (base + j) % SUBLANES
            wj = w_ref[j]
            for gi in range(CONV_GROUPS):
                a0 = r0 + gi * SUBLANES + base + j - phase
                if phase == 0:
                    src = ext_ref[a0:a0 + SUBLANES, :]
                else:
                    src = sh_ref[phase - 1, a0:a0 + SUBLANES, :]
                accs[gi] = accs[gi] + wj * src
        for gi in range(CONV_GROUPS):
            acc = accs[gi]
            mu = jnp.mean(acc, axis=-1, keepdims=True)
            d = acc - mu
            var = jnp.mean(d * d, axis=-1, keepdims=True)
            y = d * lax.rsqrt(var + EPS) * g_ref[...] + b_ref[...]
            rows = slice(r0 + gi * SUBLANES, r0 + (gi + 1) * SUBLANES)
            o_ref[rows, :] = (y * _sigmoid(y)).astype(o_ref.dtype)


def _inproj_kernel(*refs, tm, tiles_per_seq, n_cast):
    (x_ref, g_ref, w0, w1, w2, w3, w4, w5, w6, bu_ref, cw_ref, cb_ref, lg_ref, lb_ref) = refs[:14]
    cast_in = refs[14:14 + n_cast]
    lat_ref, sg_ref, cv_ref = refs[14 + n_cast:17 + n_cast]
    cast_out = refs[17 + n_cast:17 + 2 * n_cast]
    ext_ref, unew_ref, sh_ref = refs[17 + 2 * n_cast:]
    i = pl.program_id(0)

    @pl.when(i == 0)
    def _():
        ext_ref[...] = jnp.zeros(ext_ref.shape, F32)

    for src, dst in zip(cast_in, cast_out):
        dst[...] = src[...].astype(dst.dtype)

    _conv_ln_swish(ext_ref, sh_ref, cw_ref, cb_ref, lg_ref, lb_ref, cv_ref, tm)

    h = _rms(x_ref[...], g_ref[...]).astype(BF16)
    lat_ref[...] = jnp.dot(h, w0[...], preferred_element_type=F32)
    a = jnp.dot(h, w1[...], preferred_element_type=F32) + bu_ref[:, :CONV_CH]
    gt = jnp.dot(h, w2[...], preferred_element_type=F32) + bu_ref[:, CONV_CH:]
    unew_ref[...] = a * _sigmoid(gt)
    for c, w in enumerate((w3, w4, w5, w6)):
        acc = jnp.dot(h, w[...], preferred_element_type=F32)
        sg_ref[:, c * W_BLK:(c + 1) * W_BLK] = _sigmoid(acc).astype(sg_ref.dtype)

    @pl.when(i % tiles_per_seq == 0)
    def _():
        ext_ref[0:HALO, :] = jnp.zeros((HALO, CONV_CH), F32)

    @pl.when(i % tiles_per_seq != 0)
    def _():
        ext_ref[0:HALO, :] = ext_ref[tm:tm + HALO, :]

    ext_ref[HALO:, :] = unew_ref[...]


def in_proj_conv(x, g, w_all, b_u, w_dw, b_dw, ln_g, ln_b, cast_weights, seq, tm):
    m, d = x.shape
    n = m // tm
    cur = lambda w: pl.BlockSpec((tm, w), lambda i: (jnp.minimum(i, n - 1), 0))
    prev = lambda w: pl.BlockSpec((tm, w), lambda i: (jnp.maximum(i - 1, 0), 0))
    wblk = lambda c: pl.BlockSpec((d, W_BLK), lambda i: (0, c), pipeline_mode=pl.Buffered(1))
    cast_specs = [pl.BlockSpec((w.shape[0] // n, w.shape[1]), lambda i: (jnp.minimum(i, n - 1), 0))
                  for w in cast_weights]
    w8 = jnp.broadcast_to(w_dw[:, None, :], (CONV_WIDTH, SUBLANES, CONV_CH))
    vec = lambda a: jnp.broadcast_to(a.reshape(1, CONV_CH), (SUBLANES, CONV_CH))
    outs = pl.pallas_call(
        functools.partial(_inproj_kernel, tm=tm, tiles_per_seq=seq // tm,
                          n_cast=len(cast_weights)),
        grid=(n + 1,),
        in_specs=[cur(d), _resident((1, d))] + [wblk(c) for c in range(N_WBLK)]
                 + [_resident((1, 2 * CONV_CH)), _resident((CONV_WIDTH, SUBLANES, CONV_CH)),
                    _resident((SUBLANES, CONV_CH)), _resident((SUBLANES, CONV_CH)),
                    _resident((SUBLANES, CONV_CH))] + cast_specs,
        out_specs=[cur(W_BLK), cur(4 * W_BLK), prev(CONV_CH)] + cast_specs,
        out_shape=[jax.ShapeDtypeStruct((m, W_BLK), F32),
                   jax.ShapeDtypeStruct((m, 4 * W_BLK), BF16),
                   jax.ShapeDtypeStruct((m, CONV_CH), BF16)]
                  + [jax.ShapeDtypeStruct(w.shape, BF16) for w in cast_weights],
        scratch_shapes=[pltpu.VMEM((tm + HALO, CONV_CH), F32),
                        pltpu.VMEM((tm, CONV_CH), F32),
                        pltpu.VMEM((SUBLANES - 1, tm + HALO - SUBLANES, CONV_CH), F32)],
        compiler_params=_cparams(1),
        name="in_proj_conv",
    )(x, g.reshape(1, d), *([w_all] * N_WBLK), b_u.reshape(1, 2 * CONV_CH),
      w8, vec(b_dw), vec(ln_g), vec(ln_b), *cast_weights)
    return outs[0], outs[1], outs[2], outs[3:]


def _qkv_kernel(cq_ref, gq_ref, wqt_ref, c_ref, g_ref, kl_ref, tab_ref, wk_ref,
                wvt_ref, qt_ref, k_ref, vt_ref):
    cqn = _rms(cq_ref[...], gq_ref[...]).astype(BF16)
    c0 = (QK_NOPE + QK_ROPE) ** -0.5 * math.log2(math.e)
    tab_t = tab_ref[...].T * c0
    for pair in range(MLA_HEADS // 4):
        p0 = pair * 4 * HEAD_PAD
        acc4 = lax.dot_general(wqt_ref[p0:p0 + 4 * HEAD_PAD, :], cqn, _NT,
                               preferred_element_type=F32)
        for sub in range(4):
            r0 = p0 + sub * HEAD_PAD
            acc = acc4[sub * HEAD_PAD:(sub + 1) * HEAD_PAD]
            qt_ref[r0:r0 + QK_NOPE, :] = (acc[:QK_NOPE] * c0).astype(qt_ref.dtype)
            qt_ref[r0 + QK_NOPE:r0 + HEAD_PAD, :] = (acc[QK_NOPE:] * tab_t).astype(qt_ref.dtype)

    cn = _rms(c_ref[...], g_ref[...]).astype(BF16)
    prod = kl_ref[...] * tab_ref[...]
    krot2 = (prod + pltpu.roll(prod, QK_ROPE, 1)).astype(k_ref.dtype)
    for pair in range(MLA_HEADS // 2):
        acc = jnp.dot(cn, wk_ref[:, pair * 256:(pair + 1) * 256],
                      preferred_element_type=F32)
        for sub in range(2):
            hh = 2 * pair + sub
            k_ref[:, hh * HEAD_PAD:hh * HEAD_PAD + QK_NOPE] = (
                acc[:, sub * QK_NOPE:(sub + 1) * QK_NOPE].astype(k_ref.dtype))
            k_ref[:, hh * HEAD_PAD + QK_NOPE:(hh + 1) * HEAD_PAD] = krot2
    vt = lax.dot_general(wvt_ref[...], cn, _NT, preferred_element_type=F32).astype(vt_ref.dtype)
    ones = jnp.ones((BF16_ROWS, vt.shape[1]), vt_ref.dtype)
    for hh in range(MLA_HEADS):
        vt_ref[hh * VT_ROWS:hh * VT_ROWS + V_HEAD, :] = vt[hh * V_HEAD:(hh + 1) * V_HEAD, :]
        vt_ref[hh * VT_ROWS + V_HEAD:(hh + 1) * VT_ROWS, :] = ones


def proj_qkv(lat, g_q, w_qt, g_kv, w_k, w_vt, tab_k, tm):
    m = lat.shape[0]
    ckv_blk = Q_LORA // KV_LORA
    kl_blk = (Q_LORA + KV_LORA) // LANES
    nq, nk, nv = MLA_HEADS * HEAD_PAD, MLA_HEADS * HEAD_PAD, MLA_HEADS * VT_ROWS
    return pl.pallas_call(
        _qkv_kernel,
        grid=(m // tm,),
        in_specs=[pl.BlockSpec((tm, Q_LORA), lambda i: (i, 0)),
                  _resident((1, Q_LORA)),
                  _resident(w_qt.shape),
                  pl.BlockSpec((tm, KV_LORA), lambda i: (i, ckv_blk)),
                  _resident((1, KV_LORA)),
                  pl.BlockSpec((tm, LANES), lambda i: (i, kl_blk)),
                  pl.BlockSpec((tm, LANES), lambda i: (i, 0)),
                  _resident(w_k.shape), _resident(w_vt.shape)],
        out_specs=[pl.BlockSpec((nq, tm), lambda i: (0, i)),
                   pl.BlockSpec((tm, nk), lambda i: (i, 0)),
                   pl.BlockSpec((nv, tm), lambda i: (0, i))],
        out_shape=[jax.ShapeDtypeStruct((nq, m), BF16),
                   jax.ShapeDtypeStruct((m, nk), BF16),
                   jax.ShapeDtypeStruct((nv, m), BF16)],
        compiler_params=_cparams(1),
        name="proj_qkv",
    )(lat, g_q.reshape(1, Q_LORA), w_qt,
      lat, g_kv.reshape(1, KV_LORA), lat, tab_k, w_k, w_vt)


ATT_TQ = 256
ATT_HEADS = 4
ATT_LOOKAHEAD = 4


def _attn_kernel(qt_ref, k_ref, vt_ref, o_ref, *, seq):
    tq = ATT_TQ
    g_diag = tq // SUBLANES
    key_chunk = lax.broadcasted_iota(jnp.int32, (g_diag, SUBLANES, tq), 0) // (CHUNK // SUBLANES)
    qry_chunk = lax.broadcasted_iota(jnp.int32, (g_diag, SUBLANES, tq), 2) // CHUNK
    diag_mask = key_chunk <= qry_chunk

    def scores(hh, qi):
        kend = (qi + 1) * tq
        s = jnp.dot(k_ref[0:kend, hh * HEAD_PAD:(hh + 1) * HEAD_PAD],
                    qt_ref[hh * HEAD_PAD:(hh + 1) * HEAD_PAD, qi * tq:kend],
                    preferred_element_type=F32)
        return s.reshape(kend // SUBLANES, SUBLANES, tq)

    stages = [(hh, qi) for qi in range(seq // tq) for hh in range(ATT_HEADS)]
    pending = [scores(*st) for st in stages[:ATT_LOOKAHEAD]]
    for idx, (hh, qi) in enumerate(stages):
        q0, kend = qi * tq, (qi + 1) * tq
        s = pending.pop(0)
        if idx + ATT_LOOKAHEAD < len(stages):
            pending.append(scores(*stages[idx + ATT_LOOKAHEAD]))
        g_past = q0 // SUBLANES
        s_d = jnp.where(diag_mask, s[g_past:], NEG_INF)
        m8 = jnp.max(s_d, axis=0)
        if qi > 0:
            m8 = jnp.maximum(m8, jnp.max(s[:g_past], axis=0))
        m8 = jnp.broadcast_to(jnp.max(m8, axis=0, keepdims=True), (SUBLANES, tq))
        p = jnp.exp2(s_d - m8[None])
        if qi > 0:
            p = jnp.concatenate([jnp.exp2(s[:g_past] - m8[None]), p], axis=0)
        p = p.reshape(kend, tq).astype(BF16)
        ot = jnp.dot(vt_ref[hh * VT_ROWS:(hh + 1) * VT_ROWS, 0:kend], p,
                     preferred_element_type=F32)
        inv = 1.0 / ot[V_HEAD:V_HEAD + 1, :]
        o_ref[q0:kend, hh * V_HEAD:(hh + 1) * V_HEAD] = (
            (ot[:V_HEAD, :] * inv).T.astype(o_ref.dtype))


def attention(qt, k, vt, batch, seq):
    nh = ATT_HEADS
    return pl.pallas_call(
        functools.partial(_attn_kernel, seq=seq),
        grid=(batch, MLA_HEADS // nh),
        in_specs=[pl.BlockSpec((nh * HEAD_PAD, seq), lambda b, h: (h, b)),
                  pl.BlockSpec((seq, nh * HEAD_PAD), lambda b, h: (b, h)),
                  pl.BlockSpec((nh * VT_ROWS, seq), lambda b, h: (h, b))],
        out_specs=pl.BlockSpec((seq, nh * V_HEAD), lambda b, h: (b, h)),
        out_shape=jax.ShapeDtypeStruct((batch * seq, MLA_HEADS * V_HEAD), BF16),
        compiler_params=_cparams(2),
        name="mla_attention",
    )(qt, k, vt)


def _mix_kernel(v_ref, ya_ref, sga_ref, sgb_ref, x_ref, wpw_ref, bpw_ref, wo_ref, g_ref,
                x1_ref, h1_ref):
    v = v_ref[...]
    half = x_ref.shape[1] // 2
    halves = [slice(c * half, (c + 1) * half) for c in range(2)]
    ybs = [jnp.dot(v, wpw_ref[:, cols], preferred_element_type=F32) + bpw_ref[:, cols]
           for cols in halves]
    x1 = x_ref[...]
    for cols, yb in zip(halves, ybs):
        merged = (sga_ref[:, cols].astype(F32) * ya_ref[:, cols].astype(F32)
                  + sgb_ref[:, cols].astype(F32) * yb)
        x1 = x1 + jnp.dot(merged.astype(BF16), wo_ref[cols, :], preferred_element_type=F32)
    x1_ref[...] = x1
    h1_ref[...] = (x1 * g_ref[...]).astype(h1_ref.dtype)


def mix(v, ya, sg, x, w_pw2, b_pw2, w_o, g, tm):
    m, d = x.shape
    row = lambda w: pl.BlockSpec((tm, w), lambda i: (i, 0))
    return pl.pallas_call(
        _mix_kernel,
        grid=(m // tm,),
        in_specs=[row(CONV_CH), row(d),
                  pl.BlockSpec((tm, d), lambda i: (i, 0)),
                  pl.BlockSpec((tm, d), lambda i: (i, 1)),
                  row(d),
                  _resident((CONV_CH, d)), _resident((1, d)), _resident((d, d)),
                  _resident((1, d))],
        out_specs=[row(d), row(d)],
        out_shape=[jax.ShapeDtypeStruct((m, d), F32), jax.ShapeDtypeStruct((m, d), BF16)],
        compiler_params=_cparams(1),
        name="mix_out_proj",
    )(v, ya, sg, sg, x, w_pw2, b_pw2.reshape(1, d), w_o, g.reshape(1, d))


def _memkv_kernel(m_ref, g_ref, w_ref, o_ref, mn_ref):
    @pl.when(pl.program_id(0) == 0)
    def _():
        mn_ref[...] = _rms(m_ref[...], g_ref[...]).astype(mn_ref.dtype)

    o_ref[...] = jnp.dot(mn_ref[...], w_ref[...], preferred_element_type=F32).astype(o_ref.dtype)


def mem_kv(mem, g, w, tn):
    m, k = mem.shape
    n = w.shape[1]
    return pl.pallas_call(
        _memkv_kernel,
        grid=(n // tn,),
        in_specs=[_resident((m, k)), _resident((1, k)),
                  pl.BlockSpec((k, tn), lambda j: (0, j))],
        out_specs=pl.BlockSpec((m, tn), lambda j: (0, j)),
        out_shape=jax.ShapeDtypeStruct((m, n), BF16),
        scratch_shapes=[pltpu.VMEM((m, k), BF16)],
        compiler_params=_cparams(1),
        name="mem_kv",
    )(mem, g.reshape(1, k), w)


def _cross_kernel(*refs, n_cast):
    h_ref, x_ref, kv_ref, wq_ref, wo_ref, g_ref = refs[:6]
    cast_in = refs[6:6 + n_cast]
    x2_ref, h2_ref = refs[6 + n_cast:8 + n_cast]
    cast_out = refs[8 + n_cast:8 + 2 * n_cast]

    for src, dst in zip(cast_in, cast_out):
        dst[...] = src[...].astype(dst.dtype)

    h = h_ref[...]
    scale = MEM_HEAD_DIM ** -0.5
    head = lambda hd: slice(hd * MEM_HEAD_DIM, (hd + 1) * MEM_HEAD_DIM)

    x1 = x_ref[...]
    inv_rms = lax.rsqrt(jnp.mean(x1 * x1, axis=-1, keepdims=True) + EPS)

    def scores(hd):
        q = jnp.dot(h, wq_ref[:, head(hd)], preferred_element_type=F32) * inv_rms
        return lax.dot_general(q.astype(BF16), kv_ref[:, head(hd)], _NT,
                               preferred_element_type=F32) * scale

    x2 = x1
    nxt = scores(0)
    for hd in range(MEM_HEADS):
        vsl = slice(D_MODEL + hd * MEM_HEAD_DIM, D_MODEL + (hd + 1) * MEM_HEAD_DIM)
        s = nxt
        if hd + 1 < MEM_HEADS:
            nxt = scores(hd + 1)
        m = jnp.max(s, axis=-1, keepdims=True)
        p = jnp.exp(s - m)
        l = jnp.sum(p, axis=-1, keepdims=True)
        o = jnp.dot(p.astype(BF16), kv_ref[:, vsl], preferred_element_type=F32) * (1.0 / l)
        x2 = x2 + jnp.dot(o.astype(BF16), wo_ref[head(hd), :], preferred_element_type=F32)
    x2_ref[...] = x2
    h2_ref[...] = _rms(x2, g_ref[...]).astype(h2_ref.dtype)


def _cast_slice_rows(total_rows, n_steps):
    rows = -(-total_rows // n_steps)
    while total_rows % rows or rows % BF16_ROWS:
        rows += 1
    return rows


def cross_attention(h1, x1, kvm, w_q, w_o, g, cast_weights, batch, seq, mem_len, tm):
    d = D_MODEL
    nt = seq // tm
    row = pl.BlockSpec((tm, d), lambda b, s: (b * nt + s, 0))

    def cast_spec(w):
        rows = _cast_slice_rows(w.shape[0], batch * nt)
        last = w.shape[0] // rows - 1
        return pl.BlockSpec((rows, w.shape[1]), lambda b, s: (jnp.minimum(b * nt + s, last), 0))

    cast_specs = [cast_spec(w) for w in cast_weights]
    outs = pl.pallas_call(
        functools.partial(_cross_kernel, n_cast=len(cast_weights)),
        grid=(batch, nt),
        in_specs=[row, row,
                  pl.BlockSpec((mem_len, 2 * d), lambda b, s: (b, 0)),
                  _resident((d, d)), _resident((d, d)), _resident((1, d))] + cast_specs,
        out_specs=[row, row] + cast_specs,
        out_shape=[jax.ShapeDtypeStruct((batch * seq, d), F32),
                   jax.ShapeDtypeStruct((batch * seq, d), BF16)]
                  + [jax.ShapeDtypeStruct(w.shape, BF16) for w in cast_weights],
        compiler_params=_cparams(2),
        name="cross_attention",
    )(h1, x1, kvm, w_q, w_o, g.reshape(1, d), *cast_weights)
    return outs[0], outs[1], outs[2:]


FFN_TN = 512


def _ffn_up_kernel(h_ref, wg_ref, wu_ref, o_ref):
    h = h_ref[...]
    gt = jnp.dot(h, wg_ref[...], preferred_element_type=F32)
    up = jnp.dot(h, wu_ref[...], preferred_element_type=F32)
    o_ref[...] = (gt * _sigmoid(gt) * up).astype(o_ref.dtype)


def ffn_up(h, w_gate, w_up, tm):
    m, k = h.shape
    n = w_gate.shape[1]
    wspec = pl.BlockSpec((k, FFN_TN), lambda j, i: (0, j))
    return pl.pallas_call(
        _ffn_up_kernel,
        grid=(n // FFN_TN, m // tm),
        in_specs=[pl.BlockSpec((tm, k), lambda j, i: (i, 0)), wspec, wspec],
        out_specs=pl.BlockSpec((tm, FFN_TN), lambda j, i: (i, j)),
        out_shape=jax.ShapeDtypeStruct((m, n), BF16),
        compiler_params=_cparams(2),
        name="ffn_up",
    )(h, w_gate, w_up)


def _ffn_down_kernel(a_ref, w_ref, x_ref, g_ref, o_ref):
    x3 = x_ref[...] + jnp.dot(a_ref[...], w_ref[...], preferred_element_type=F32)
    o_ref[...] = _rms(x3, g_ref[...])


def ffn_down(a, w, x, g, tm):
    m, kdim = a.shape
    d = w.shape[1]
    return pl.pallas_call(
        _ffn_down_kernel,
        grid=(m // tm,),
        in_specs=[pl.BlockSpec((tm, kdim), lambda i: (i, 0)),
                  _resident((kdim, d)),
                  pl.BlockSpec((tm, d), lambda i: (i, 0)),
                  _resident((1, d))],
        out_specs=pl.BlockSpec((tm, d), lambda i: (i, 0)),
        out_shape=jax.ShapeDtypeStruct((m, d), F32),
        compiler_params=_cparams(1),
        name="ffn_down",
    )(a, w, x, g.reshape(1, d))


def _swap_halves(w):
    half = w.shape[-1] // 2
    return jnp.concatenate([w[..., half:], w[..., :half]], axis=-1)


PACK_TAIL = 256


def _pack_w_kernel(tail_ref, blk_ref, o_ref):
    j = pl.program_id(0)
    o_kr = Q_LORA + KV_LORA
    half = QK_ROPE // 2
    lat_src = o_kr + QK_ROPE

    @pl.when(j == 0)
    def _():
        b = blk_ref[...]
        rows = jnp.concatenate(
            [b[0:lat_src], b[o_kr + half:lat_src], b[o_kr:o_kr + half],
             jnp.zeros((W_BLK - LAT_W, b.shape[1]), b.dtype)], axis=0)
        o_ref[...] = rows.T.astype(o_ref.dtype)

    @pl.when(j > 0)
    def _():
        n_tail = W_BLK - lat_src
        rows = jnp.concatenate(
            [tail_ref[PACK_TAIL - n_tail:PACK_TAIL, :], blk_ref[0:lat_src, :]], axis=0)
        o_ref[...] = rows.T.astype(o_ref.dtype)


def pack_w_in(w_t):
    n_in, d = w_t.shape
    tails_per_blk = W_BLK // PACK_TAIL
    return pl.pallas_call(
        _pack_w_kernel,
        grid=(N_WBLK,),
        in_specs=[pl.BlockSpec((PACK_TAIL, d),
                               lambda j: (jnp.maximum(j, 1) * tails_per_blk - 1, 0)),
                  pl.BlockSpec((W_BLK, d), lambda j: (j, 0))],
        out_specs=pl.BlockSpec((d, W_BLK), lambda j: (0, j)),
        out_shape=jax.ShapeDtypeStruct((d, N_WBLK * W_BLK), BF16),
        compiler_params=_cparams(1),
        name="pack_w_in",
    )(w_t, w_t)


def _rope_table(positions):
    inv_freq = 1.0 / (ROPE_THETA ** (jnp.arange(0, QK_ROPE, 2, dtype=F32) / QK_ROPE))
    ang = positions.astype(F32).reshape(-1, 1) * inv_freq
    cos, sin = jnp.cos(ang), jnp.sin(ang)
    return jnp.concatenate([cos, cos, -sin, sin], axis=-1)


def kernel(x, mem, positions, norm_mix, w_in, b_conv_in, norm_cq, w_uq, norm_ckv, w_ukv, w_dw, b_dw, ln_conv_g, ln_conv_b, w_pw2, b_pw2, w_o, norm_cross, norm_mem, w_q_mem, w_kv_mem, w_o_mem, norm_ffn, w_gate, w_up, w_down, norm_final):
    batch, seq, d = x.shape
    mem_len = mem.shape[1]
    t = batch * seq
    xf = x.reshape(t, d)
    tab_k = _rope_table(positions)
    l = 0

    w_all = pack_w_in(w_in[l].T)
    wq3 = w_uq[l].reshape(Q_LORA, MLA_HEADS, QK_NOPE + QK_ROPE)
    w_qt = jnp.concatenate(
        [wq3, _swap_halves(wq3[..., QK_NOPE:])], axis=-1).reshape(Q_LORA, -1).T.astype(BF16)
    wkv3 = w_ukv[l].reshape(KV_LORA, MLA_HEADS, QK_NOPE + V_HEAD)
    w_k = wkv3[..., :QK_NOPE].reshape(KV_LORA, -1).astype(BF16)
    w_vt = wkv3[..., QK_NOPE:].reshape(KV_LORA, -1).T.astype(BF16)

    later_weights = (w_pw2[l], w_o[l], w_q_mem[l], w_o_mem[l], w_kv_mem[l])
    lat, sg, cv, (w_pw2_b, w_o_b, w_qm_b, w_om_b, w_kvm_b) = in_proj_conv(
        xf, norm_mix[l], w_all, b_conv_in[l], w_dw[l], b_dw[l], ln_conv_g[l], ln_conv_b[l],
        later_weights, seq, tm=256)
    qt, k, vt = proj_qkv(lat, norm_cq[l], w_qt, norm_ckv[l], w_k, w_vt, tab_k, tm=512)
    y_a = attention(qt, k, vt, batch, seq)
    x1, h1 = mix(cv, y_a, sg, xf, w_pw2_b, b_pw2[l], w_o_b, norm_cross[l], tm=256)

    kvm = mem_kv(mem.reshape(batch * mem_len, d), norm_mem[l], w_kvm_b, tn=1024)
    x2, h2, (w_gate_b, w_up_b, w_down_b) = cross_attention(
        h1, x1, kvm, w_qm_b, w_om_b, norm_ffn[l], (w_gate[l], w_up[l], w_down[l]),
        batch, seq, mem_len, tm=256)

    a = ffn_up(h2, w_gate_b, w_up_b, tm=1024)
    out = ffn_down(a, w_down_b, x2, norm_final, tm=256)
    return out.reshape(batch, seq, d)
```

```python
import functools
import math

import jax
import jax.numpy as jnp
from jax import lax
from jax.experimental import pallas as pl
from jax.experimental.pallas import tpu as pltpu

F32 = jnp.float32
BF16 = jnp.bfloat16

D_MODEL = 2048
CHUNK = 64
MLA_HEADS = 16
QK_NOPE = 128
QK_ROPE = 64
V_HEAD = 128
Q_LORA = 512
KV_LORA = 256
ROPE_THETA = 10000.0
CONV_CH = 1024
CONV_WIDTH = 31
MEM_HEADS = 4
MEM_HEAD_DIM = D_MODEL // MEM_HEADS
EPS = 1e-6
NEG_INF = -1e30

LANES = 128
SUBLANES = 8
HEAD_PAD = 256
BF16_ROWS = 16
VT_ROWS = V_HEAD + BF16_ROWS
LAT_W = Q_LORA + KV_LORA + 2 * QK_ROPE
HALO = 32
VMEM_LIMIT = 56 * 1024 * 1024

_NT = (((1,), (1,)), ((), ()))


def _cparams(n_axes):
    return pltpu.CompilerParams(
        dimension_semantics=("arbitrary",) * n_axes,
        vmem_limit_bytes=VMEM_LIMIT)


def _sigmoid(x):
    return 1.0 / (1.0 + jnp.exp(-x))


def _rms(x, g):
    ms = jnp.mean(x * x, axis=-1, keepdims=True)
    return x * lax.rsqrt(ms + EPS) * g


def _resident(shape):
    zeros = (0,) * len(shape)
    return pl.BlockSpec(shape, lambda *_: zeros, pipeline_mode=pl.Buffered(1))


W_BLK = 1024
N_WBLK = 7
CONV_GROUPS = 2


def _conv_ln_swish(ext_ref, sh_ref, w_ref, bdw_ref, g_ref, b_ref, o_ref, tm):
    n_sh = tm + HALO - SUBLANES
    for r in range(1, SUBLANES):
        sh_ref[r - 1, :, :] = ext_ref[r:r + n_sh, :]
    base = HALO - (CONV_WIDTH - 1)
    taps = sorted(range(CONV_WIDTH), key=lambda j: ((base + j) % SUBLANES, j))
    for c in range(tm // (SUBLANES * CONV_GROUPS)):
        r0 = c * SUBLANES * CONV_GROUPS
        accs = [bdw_ref[...]] * CONV_GROUPS
        for j in taps:
            phase = (base + j) % SUBLANES
            wj = w_ref[j]
            for gi in range(CONV_GROUPS):
                a0 = r0 + gi * SUBLANES + base + j - phase
                if phase == 0:
                    src = ext_ref[a0:a0 + SUBLANES, :]
                else:
                    src = sh_ref[phase - 1, a0:a0 + SUBLANES, :]
                accs[gi] = accs[gi] + wj * src
        for gi in range(CONV_GROUPS):
            acc = accs[gi]
            mu = jnp.mean(acc, axis=-1, keepdims=True)
            d = acc - mu
            var = jnp.mean(d * d, axis=-1, keepdims=True)
            y = d * lax.rsqrt(var + EPS) * g_ref[...] + b_ref[...]
            rows = slice(r0 + gi * SUBLANES, r0 + (gi + 1) * SUBLANES)
            o_ref[rows, :] = (y * _sigmoid(y)).astype(o_ref.dtype)


def _inproj_kernel(*refs, tm, tiles_per_seq, n_cast):
    (x_ref, g_ref, w0, w1, w2, w3, w4, w5, w6, bu_ref, cw_ref, cb_ref, lg_ref, lb_ref) = refs[:14]
    cast_in = refs[14:14 + n_cast]
    lat_ref, sg_ref, cv_ref = refs[14 + n_cast:17 + n_cast]
    cast_out = refs[17 + n_cast:17 + 2 * n_cast]
    ext_ref, unew_ref, sh_ref = refs[17 + 2 * n_cast:]
    i = pl.program_id(0)

    @pl.when(i == 0)
    def _():
        ext_ref[...] = jnp.zeros(ext_ref.shape, F32)

    for src, dst in zip(cast_in, cast_out):
        dst[...] = src[...].astype(dst.dtype)

    _conv_ln_swish(ext_ref, sh_ref, cw_ref, cb_ref, lg_ref, lb_ref, cv_ref, tm)

    h = _rms(x_ref[...], g_ref[...]).astype(BF16)
    lat_ref[...] = jnp.dot(h, w0[...], preferred_element_type=F32)
    a = jnp.dot(h, w1[...], preferred_element_type=F32) + bu_ref[:, :CONV_CH]
    gt = jnp.dot(h, w2[...], preferred_element_type=F32) + bu_ref[:, CONV_CH:]
    unew_ref[...] = a * _sigmoid(gt)
    for c, w in enumerate((w3, w4, w5, w6)):
        acc = jnp.dot(h, w[...], preferred_element_type=F32)
        sg_ref[:, c * W_BLK:(c + 1) * W_BLK] = _sigmoid(acc).astype(sg_ref.dtype)

    @pl.when(i % tiles_per_seq == 0)
    def _():
        ext_ref[0:HALO, :] = jnp.zeros((HALO, CONV_CH), F32)

    @pl.when(i % tiles_per_seq != 0)
    def _():
        ext_ref[0:HALO, :] = ext_ref[tm:tm + HALO, :]

    ext_ref[HALO:, :] = unew_ref[...]


def in_proj_conv(x, g, w_all, b_u, w_dw, b_dw, ln_g, ln_b, cast_weights, seq, tm):
    m, d = x.shape
    n = m // tm
    cur = lambda w: pl.BlockSpec((tm, w), lambda i: (jnp.minimum(i, n - 1), 0))
    prev = lambda w: pl.BlockSpec((tm, w), lambda i: (jnp.maximum(i - 1, 0), 0))
    wblk = lambda c: pl.BlockSpec((d, W_BLK), lambda i: (0, c), pipeline_mode=pl.Buffered(1))
    cast_specs = [pl.BlockSpec((w.shape[0] // n, w.shape[1]), lambda i: (jnp.minimum(i, n - 1), 0))
                  for w in cast_weights]
    w8 = jnp.broadcast_to(w_dw[:, None, :], (CONV_WIDTH, SUBLANES, CONV_CH))
    vec = lambda a: jnp.broadcast_to(a.reshape(1, CONV_CH), (SUBLANES, CONV_CH))
    outs = pl.pallas_call(
        functools.partial(_inproj_kernel, tm=tm, tiles_per_seq=seq // tm,
                          n_cast=len(cast_weights)),
        grid=(n + 1,),
        in_specs=[cur(d), _resident((1, d))] + [wblk(c) for c in range(N_WBLK)]
                 + [_resident((1, 2 * CONV_CH)), _resident((CONV_WIDTH, SUBLANES, CONV_CH)),
                    _resident((SUBLANES, CONV_CH)), _resident((SUBLANES, CONV_CH)),
                    _resident((SUBLANES, CONV_CH))] + cast_specs,
        out_specs=[cur(W_BLK), cur(4 * W_BLK), prev(CONV_CH)] + cast_specs,
        out_shape=[jax.ShapeDtypeStruct((m, W_BLK), F32),
                   jax.ShapeDtypeStruct((m, 4 * W_BLK), BF16),
                   jax.ShapeDtypeStruct((m, CONV_CH), BF16)]
                  + [jax.ShapeDtypeStruct(w.shape, BF16) for w in cast_weights],
        scratch_shapes=[pltpu.VMEM((tm + HALO, CONV_CH), F32),
                        pltpu.VMEM((tm, CONV_CH), F32),
                        pltpu.VMEM((SUBLANES - 1, tm + HALO - SUBLANES, CONV_CH), F32)],
        compiler_params=_cparams(1),
        name="in_proj_conv",
    )(x, g.reshape(1, d), *([w_all] * N_WBLK), b_u.reshape(1, 2 * CONV_CH),
      w8, vec(b_dw), vec(ln_g), vec(ln_b), *cast_weights)
    return outs[0], outs[1], outs[2], outs[3:]


def _qkv_kernel(cq_ref, gq_ref, wqt_ref, c_ref, g_ref, kl_ref, tab_ref, wk_ref,
                wvt_ref, qt_ref, k_ref, vt_ref):
    cqn = _rms(cq_ref[...], gq_ref[...]).astype(BF16)
    c0 = (QK_NOPE + QK_ROPE) ** -0.5 * math.log2(math.e)
    tab_t = tab_ref[...].T * c0
    for hh in range(MLA_HEADS):
        r0 = hh * HEAD_PAD
        acc = lax.dot_general(wqt_ref[r0:r0 + HEAD_PAD, :], cqn, _NT,
                              preferred_element_type=F32)
        qt_ref[r0:r0 + QK_NOPE, :] = (acc[:QK_NOPE] * c0).astype(qt_ref.dtype)
        qt_ref[r0 + QK_NOPE:r0 + HEAD_PAD, :] = (acc[QK_NOPE:] * tab_t).astype(qt_ref.dtype)

    cn = _rms(c_ref[...], g_ref[...]).astype(BF16)
    prod = kl_ref[...] * tab_ref[...]
    krot2 = (prod + pltpu.roll(prod, QK_ROPE, 1)).astype(k_ref.dtype)
    for pair in range(MLA_HEADS // 2):
        acc = jnp.dot(cn, wk_ref[:, pair * 256:(pair + 1) * 256],
                      preferred_element_type=F32)
        for sub in range(2):
            c0k = sub * HEAD_PAD
            k_ref[pair, :, c0k:c0k + QK_NOPE] = (
                acc[:, sub * QK_NOPE:(sub + 1) * QK_NOPE].astype(k_ref.dtype))
            k_ref[pair, :, c0k + QK_NOPE:c0k + HEAD_PAD] = krot2
    vt = lax.dot_general(wvt_ref[...], cn, _NT, preferred_element_type=F32).astype(vt_ref.dtype)
    ones = jnp.ones((BF16_ROWS, vt.shape[1]), vt_ref.dtype)
    for hh in range(MLA_HEADS):
        vt_ref[hh * VT_ROWS:hh * VT_ROWS + V_HEAD, :] = vt[hh * V_HEAD:(hh + 1) * V_HEAD, :]
        vt_ref[hh * VT_ROWS + V_HEAD:(hh + 1) * VT_ROWS, :] = ones


def proj_qkv(lat, g_q, w_qt, g_kv, w_k, w_vt, tab_k, tm):
    m = lat.shape[0]
    ckv_blk = Q_LORA // KV_LORA
    kl_blk = (Q_LORA + KV_LORA) // LANES
    nq, nv = MLA_HEADS * HEAD_PAD, MLA_HEADS * VT_ROWS
    return pl.pallas_call(
        _qkv_kernel,
        grid=(m // tm,),
        in_specs=[pl.BlockSpec((tm, Q_LORA), lambda i: (i, 0)),
                  _resident((1, Q_LORA)),
                  _resident(w_qt.shape),
                  pl.BlockSpec((tm, KV_LORA), lambda i: (i, ckv_blk)),
                  _resident((1, KV_LORA)),
                  pl.BlockSpec((tm, LANES), lambda i: (i, kl_blk)),
                  pl.BlockSpec((tm, LANES), lambda i: (i, 0)),
                  _resident(w_k.shape), _resident(w_vt.shape)],
        out_specs=[pl.BlockSpec((nq, tm), lambda i: (0, i)),
                   pl.BlockSpec((MLA_HEADS // 2, tm, 2 * HEAD_PAD), lambda i: (0, i, 0)),
                   pl.BlockSpec((nv, tm), lambda i: (0, i))],
        out_shape=[jax.ShapeDtypeStruct((nq, m), BF16),
                   jax.ShapeDtypeStruct((MLA_HEADS // 2, m, 2 * HEAD_PAD), BF16),
                   jax.ShapeDtypeStruct((nv, m), BF16)],
        compiler_params=_cparams(1),
        name="proj_qkv",
    )(lat, g_q.reshape(1, Q_LORA), w_qt,
      lat, g_kv.reshape(1, KV_LORA), lat, tab_k, w_k, w_vt)


ATT_TQ = 256
ATT_HEADS = 2
ATT_LOOKAHEAD = 4


def _attn_kernel(qt_ref, k_ref, vt_ref, o_ref, *, seq):
    tq = ATT_TQ
    g_diag = tq // SUBLANES
    key_chunk = lax.broadcasted_iota(jnp.int32, (g_diag, SUBLANES, tq), 0) // (CHUNK // SUBLANES)
    qry_chunk = lax.broadcasted_iota(jnp.int32, (g_diag, SUBLANES, tq), 2) // CHUNK
    diag_mask = key_chunk <= qry_chunk

    def scores(hh, qi):
        kend = (qi + 1) * tq
        s = jnp.dot(k_ref[0:kend, hh * HEAD_PAD:(hh + 1) * HEAD_PAD],
                    qt_ref[hh * HEAD_PAD:(hh + 1) * HEAD_PAD, qi * tq:kend],
                    preferred_element_type=F32)
        return s.reshape(kend // SUBLANES, SUBLANES, tq)

    stages = [(hh, qi) for qi in range(seq // tq) for hh in range(ATT_HEADS)]
    pending = [scores(*st) for st in stages[:ATT_LOOKAHEAD]]
    for idx, (hh, qi) in enumerate(stages):
        q0, kend = qi * tq, (qi + 1) * tq
        s = pending.pop(0)
        if idx + ATT_LOOKAHEAD < len(stages):
            pending.append(scores(*stages[idx + ATT_LOOKAHEAD]))
        g_past = q0 // SUBLANES
        s_d = jnp.where(diag_mask, s[g_past:], NEG_INF)
        m8 = jnp.max(s_d, axis=0)
        if qi > 0:
            m8 = jnp.maximum(m8, jnp.max(s[:g_past], axis=0))
        m8 = jnp.broadcast_to(jnp.max(m8, axis=0, keepdims=True), (SUBLANES, tq))
        p = jnp.exp2(s_d - m8[None])
        if qi > 0:
            p = jnp.concatenate([jnp.exp2(s[:g_past] - m8[None]), p], axis=0)
        p = p.reshape(kend, tq).astype(BF16)
        ot = jnp.dot(vt_ref[hh * VT_ROWS:(hh + 1) * VT_ROWS, 0:kend], p,
                     preferred_element_type=F32)
        inv = 1.0 / ot[V_HEAD:V_HEAD + 1, :]
        o_ref[q0:kend, hh * V_HEAD:(hh + 1) * V_HEAD] = (
            (ot[:V_HEAD, :] * inv).T.astype(o_ref.dtype))


def attention(qt, k, vt, batch, seq):
    nh = ATT_HEADS
    return pl.pallas_call(
        functools.partial(_attn_kernel, seq=seq),
        grid=(batch, MLA_HEADS // nh),
        in_specs=[pl.BlockSpec((nh * HEAD_PAD, seq), lambda b, h: (h, b)),
                  pl.BlockSpec((None, seq, nh * HEAD_PAD), lambda b, h: (h, b, 0)),
                  pl.BlockSpec((nh * VT_ROWS, seq), lambda b, h: (h, b))],
        out_specs=pl.BlockSpec((None, seq, nh * V_HEAD), lambda b, h: (h, b, 0)),
        out_shape=jax.ShapeDtypeStruct((MLA_HEADS // nh, batch * seq, nh * V_HEAD), BF16),
        compiler_params=_cparams(2),
        name="mla_attention",
    )(qt, k, vt)


def _mix_kernel(*refs, n_ya):
    v_ref = refs[0]
    ya_refs = refs[1:1 + n_ya]
    sga_ref, sgb_ref, x_ref, wpw_ref, bpw_ref, wo_ref, g_ref, x1_ref, h1_ref = refs[1 + n_ya:]
    yb = jnp.dot(v_ref[...], wpw_ref[...], preferred_element_type=F32) + bpw_ref[...]
    ya = jnp.concatenate([r[...] for r in ya_refs], axis=1)
    merged = (sga_ref[...].astype(F32) * ya.astype(F32)
              + sgb_ref[...].astype(F32) * yb)
    x1 = x_ref[...] + jnp.dot(merged.astype(BF16), wo_ref[...], preferred_element_type=F32)
    x1_ref[...] = x1
    h1_ref[...] = _rms(x1, g_ref[...]).astype(h1_ref.dtype)


def mix(v, ya, sg, x, w_pw2, b_pw2, w_o, g, tm):
    m, d = x.shape
    row = lambda w: pl.BlockSpec((tm, w), lambda i: (i, 0))
    n_ya, _, w_ya = ya.shape
    ya_specs = [pl.BlockSpec((None, tm, w_ya), functools.partial(lambda c, i: (c, i, 0), c))
                for c in range(n_ya)]
    return pl.pallas_call(
        functools.partial(_mix_kernel, n_ya=n_ya),
        grid=(m // tm,),
        in_specs=[row(CONV_CH)] + ya_specs + [
                  pl.BlockSpec((tm, d), lambda i: (i, 0)),
                  pl.BlockSpec((tm, d), lambda i: (i, 1)),
                  row(d),
                  _resident((CONV_CH, d)), _resident((1, d)), _resident((d, d)),
                  _resident((1, d))],
        out_specs=[row(d), row(d)],
        out_shape=[jax.ShapeDtypeStruct((m, d), F32), jax.ShapeDtypeStruct((m, d), BF16)],
        compiler_params=_cparams(1),
        name="mix_out_proj",
    )(v, *([ya] * n_ya), sg, sg, x, w_pw2, b_pw2.reshape(1, d), w_o, g.reshape(1, d))


def _memkv_kernel(m_ref, g_ref, w_ref, o_ref, mn_ref):
    @pl.when(pl.program_id(0) == 0)
    def _():
        mn_ref[...] = _rms(m_ref[...], g_ref[...]).astype(mn_ref.dtype)

    o_ref[...] = jnp.dot(mn_ref[...], w_ref[...], preferred_element_type=F32).astype(o_ref.dtype)


def mem_kv(mem, g, w, tn):
    m, k = mem.shape
    n = w.shape[1]
    return pl.pallas_call(
        _memkv_kernel,
        grid=(n // tn,),
        in_specs=[_resident((m, k)), _resident((1, k)),
                  pl.BlockSpec((k, tn), lambda j: (0, j))],
        out_specs=pl.BlockSpec((m, tn), lambda j: (0, j)),
        out_shape=jax.ShapeDtypeStruct((m, n), BF16),
        scratch_shapes=[pltpu.VMEM((m, k), BF16)],
        compiler_params=_cparams(1),
        name="mem_kv",
    )(mem, g.reshape(1, k), w)


def _cross_kernel(*refs, n_cast):
    h_ref, x_ref, kv_ref, wq_ref, wo_ref, g_ref = refs[:6]
    cast_in = refs[6:6 + n_cast]
    x2_ref, h2_ref = refs[6 + n_cast:8 + n_cast]
    cast_out = refs[8 + n_cast:8 + 2 * n_cast]
    o_scr = refs[8 + 2 * n_cast]

    for src, dst in zip(cast_in, cast_out):
        dst[...] = src[...].astype(dst.dtype)

    qm = jnp.dot(h_ref[...], wq_ref[...], preferred_element_type=F32)
    scale = MEM_HEAD_DIM ** -0.5

    def scores(hd):
        sl = slice(hd * MEM_HEAD_DIM, (hd + 1) * MEM_HEAD_DIM)
        return lax.dot_general(qm[:, sl].astype(BF16), kv_ref[:, sl], _NT,
                               preferred_element_type=F32) * scale

    nxt = scores(0)
    for hd in range(MEM_HEADS):
        sl = slice(hd * MEM_HEAD_DIM, (hd + 1) * MEM_HEAD_DIM)
        vsl = slice(D_MODEL + hd * MEM_HEAD_DIM, D_MODEL + (hd + 1) * MEM_HEAD_DIM)
        s = nxt
        if hd + 1 < MEM_HEADS:
            nxt = scores(hd + 1)
        m = jnp.max(s, axis=-1, keepdims=True)
        p = jnp.exp(s - m)
        l = jnp.sum(p, axis=-1, keepdims=True)
        o = jnp.dot(p.astype(BF16), kv_ref[:, vsl], preferred_element_type=F32)
        o_scr[:, sl] = (o * (1.0 / l)).astype(o_scr.dtype)
    x2 = x_ref[...] + jnp.dot(o_scr[...], wo_ref[...], preferred_element_type=F32)
    x2_ref[...] = x2
    h2_ref[...] = _rms(x2, g_ref[...]).astype(h2_ref.dtype)


def _cast_slice_rows(total_rows, n_steps):
    rows = -(-total_rows // n_steps)
    while total_rows % rows or rows % BF16_ROWS:
        rows += 1
    return rows


def cross_attention(h1, x1, kvm, w_q, w_o, g, cast_weights, batch, seq, mem_len, tm):
    d = D_MODEL
    nt = seq // tm
    row = pl.BlockSpec((tm, d), lambda b, s: (b * nt + s, 0))

    def cast_spec(w):
        rows = _cast_slice_rows(w.shape[0], batch * nt)
        last = w.shape[0] // rows - 1
        return pl.BlockSpec((rows, w.shape[1]), lambda b, s: (jnp.minimum(b * nt + s, last), 0))

    cast_specs = [cast_spec(w) for w in cast_weights]
    outs = pl.pallas_call(
        functools.partial(_cross_kernel, n_cast=len(cast_weights)),
        grid=(batch, nt),
        in_specs=[row, row,
                  pl.BlockSpec((mem_len, 2 * d), lambda b, s: (b, 0)),
                  _resident((d, d)), _resident((d, d)), _resident((1, d))] + cast_specs,
        out_specs=[row, row] + cast_specs,
        out_shape=[jax.ShapeDtypeStruct((batch * seq, d), F32),
                   jax.ShapeDtypeStruct((batch * seq, d), BF16)]
                  + [jax.ShapeDtypeStruct(w.shape, BF16) for w in cast_weights],
        scratch_shapes=[pltpu.VMEM((tm, d), BF16)],
        compiler_params=_cparams(2),
        name="cross_attention",
    )(h1, x1, kvm, w_q, w_o, g.reshape(1, d), *cast_weights)
    return outs[0], outs[1], outs[2:]


FFN_TN = 512


def _ffn_up_kernel(h_ref, wg_ref, wu_ref, o_ref):
    h = h_ref[...]
    gt = jnp.dot(h, wg_ref[...], preferred_element_type=F32)
    up = jnp.dot(h, wu_ref[...], preferred_element_type=F32)
    o_ref[...] = (gt * _sigmoid(gt) * up).astype(o_ref.dtype)


def ffn_up(h, w_gate, w_up, tm):
    m, k = h.shape
    n = w_gate.shape[1]
    wspec = pl.BlockSpec((k, FFN_TN), lambda j, i: (0, j))
    return pl.pallas_call(
        _ffn_up_kernel,
        grid=(n // FFN_TN, m // tm),
        in_specs=[pl.BlockSpec((tm, k), lambda j, i: (i, 0)), wspec, wspec],
        out_specs=pl.BlockSpec((tm, FFN_TN), lambda j, i: (i, j)),
        out_shape=jax.ShapeDtypeStruct((m, n), BF16),
        compiler_params=_cparams(2),
        name="ffn_up",
    )(h, w_gate, w_up)


def _ffn_down_kernel(a_ref, w_ref, x_ref, g_ref, o_ref):
    x3 = x_ref[...] + jnp.dot(a_ref[...], w_ref[...], preferred_element_type=F32)
    o_ref[...] = _rms(x3, g_ref[...])


def ffn_down(a, w, x, g, tm):
    m, kdim = a.shape
    d = w.shape[1]
    return pl.pallas_call(
        _ffn_down_kernel,
        grid=(m // tm,),
        in_specs=[pl.BlockSpec((tm, kdim), lambda i: (i, 0)),
                  _resident((kdim, d)),
                  pl.BlockSpec((tm, d), lambda i: (i, 0)),
                  _resident((1, d))],
        out_specs=pl.BlockSpec((tm, d), lambda i: (i, 0)),
        out_shape=jax.ShapeDtypeStruct((m, d), F32),
        compiler_params=_cparams(1),
        name="ffn_down",
    )(a, w, x, g.reshape(1, d))


def _swap_halves(w):
    half = w.shape[-1] // 2
    return jnp.concatenate([w[..., half:], w[..., :half]], axis=-1)


PACK_TAIL = 256


def _pack_w_kernel(tail_ref, blk_ref, o_ref):
    j = pl.program_id(0)
    o_kr = Q_LORA + KV_LORA
    half = QK_ROPE // 2
    lat_src = o_kr + QK_ROPE

    @pl.when(j == 0)
    def _():
        b = blk_ref[...]
        rows = jnp.concatenate(
            [b[0:lat_src], b[o_kr + half:lat_src], b[o_kr:o_kr + half],
             jnp.zeros((W_BLK - LAT_W, b.shape[1]), b.dtype)], axis=0)
        o_ref[...] = rows.T.astype(o_ref.dtype)

    @pl.when(j > 0)
    def _():
        n_tail = W_BLK - lat_src
        rows = jnp.concatenate(
            [tail_ref[PACK_TAIL - n_tail:PACK_TAIL, :], blk_ref[0:lat_src, :]], axis=0)
        o_ref[...] = rows.T.astype(o_ref.dtype)


def pack_w_in(w_t):
    n_in, d = w_t.shape
    tails_per_blk = W_BLK // PACK_TAIL
    return pl.pallas_call(
        _pack_w_kernel,
        grid=(N_WBLK,),
        in_specs=[pl.BlockSpec((PACK_TAIL, d),
                               lambda j: (jnp.maximum(j, 1) * tails_per_blk - 1, 0)),
                  pl.BlockSpec((W_BLK, d), lambda j: (j, 0))],
        out_specs=pl.BlockSpec((d, W_BLK), lambda j: (0, j)),
        out_shape=jax.ShapeDtypeStruct((d, N_WBLK * W_BLK), BF16),
        compiler_params=_cparams(1),
        name="pack_w_in",
    )(w_t, w_t)


def _rope_table(positions):
    inv_freq = 1.0 / (ROPE_THETA ** (jnp.arange(0, QK_ROPE, 2, dtype=F32) / QK_ROPE))
    ang = positions.astype(F32).reshape(-1, 1) * inv_freq
    cos, sin = jnp.cos(ang), jnp.sin(ang)
    return jnp.concatenate([cos, cos, -sin, sin], axis=-1)


def kernel(x, mem, positions, norm_mix, w_in, b_conv_in, norm_cq, w_uq, norm_ckv, w_ukv, w_dw, b_dw, ln_conv_g, ln_conv_b, w_pw2, b_pw2, w_o, norm_cross, norm_mem, w_q_mem, w_kv_mem, w_o_mem, norm_ffn, w_gate, w_up, w_down, norm_final):
    batch, seq, d = x.shape
    mem_len = mem.shape[1]
    t = batch * seq
    xf = x.reshape(t, d)
    tab_k = _rope_table(positions)
    l = 0

    w_all = pack_w_in(w_in[l].T)
    wq3 = w_uq[l].reshape(Q_LORA, MLA_HEADS, QK_NOPE + QK_ROPE)
    w_qt = jnp.concatenate(
        [wq3, _swap_halves(wq3[..., QK_NOPE:])], axis=-1).reshape(Q_LORA, -1).T.astype(BF16)
    wkv3 = w_ukv[l].reshape(KV_LORA, MLA_HEADS, QK_NOPE + V_HEAD)
    w_k = wkv3[..., :QK_NOPE].reshape(KV_LORA, -1).astype(BF16)
    w_vt = wkv3[..., QK_NOPE:].reshape(KV_LORA, -1).T.astype(BF16)

    later_weights = (w_pw2[l], w_o[l], w_q_mem[l], w_o_mem[l], w_kv_mem[l])
    lat, sg, cv, (w_pw2_b, w_o_b, w_qm_b, w_om_b, w_kvm_b) = in_proj_conv(
        xf, norm_mix[l], w_all, b_conv_in[l], w_dw[l], b_dw[l], ln_conv_g[l], ln_conv_b[l],
        later_weights, seq, tm=256)
    qt, k, vt = proj_qkv(lat, norm_cq[l], w_qt, norm_ckv[l], w_k, w_vt, tab_k, tm=512)
    y_a = attention(qt, k, vt, batch, seq)
    x1, h1 = mix(cv, y_a, sg, xf, w_pw2_b, b_pw2[l], w_o_b, norm_cross[l], tm=256)

    kvm = mem_kv(mem.reshape(batch * mem_len, d), norm_mem[l], w_kvm_b, tn=1024)
    x2, h2, (w_gate_b, w_up_b, w_down_b) = cross_attention(
        h1, x1, kvm, w_qm_b, w_om_b, norm_ffn[l], (w_gate[l], w_up[l], w_down[l]),
        batch, seq, mem_len, tm=256)

    a = ffn_up(h2, w_gate_b, w_up_b, tm=1024)
    out = ffn_down(a, w_down_b, x2, norm_final, tm=256)
    return out.reshape(batch, seq, d)
```

```python
import functools
import math

import jax
import jax.numpy as jnp
from jax import lax
from jax.experimental import pallas as pl
from jax.experimental.pallas import tpu as pltpu

F32 = jnp.float32
BF16 = jnp.bfloat16

D_MODEL = 2048
CHUNK = 64
MLA_HEADS = 16
QK_NOPE = 128
QK_ROPE = 64
V_HEAD = 128
Q_LORA = 512
KV_LORA = 256
ROPE_THETA = 10000.0
CONV_CH = 1024
CONV_WIDTH = 31
MEM_HEADS = 4
MEM_HEAD_DIM = D_MODEL // MEM_HEADS
EPS = 1e-6
NEG_INF = -1e30

LANES = 128
SUBLANES = 8
HEAD_PAD = 256
BF16_ROWS = 16
VT_ROWS = V_HEAD + BF16_ROWS
LAT_W = Q_LORA + KV_LORA + 2 * QK_ROPE
HALO = 32
VMEM_LIMIT = 56 * 1024 * 1024

_NT = (((1,), (1,)), ((), ()))


def _cparams(n_axes):
    return pltpu.CompilerParams(
        dimension_semantics=("arbitrary",) * n_axes,
        vmem_limit_bytes=VMEM_LIMIT)


def _sigmoid(x):
    return 1.0 / (1.0 + jnp.exp(-x))


def _rms(x, g):
    ms = jnp.mean(x * x, axis=-1, keepdims=True)
    return x * lax.rsqrt(ms + EPS) * g


def _resident(shape):
    zeros = (0,) * len(shape)
    return pl.BlockSpec(shape, lambda *_: zeros, pipeline_mode=pl.Buffered(1))


W_BLK = 1024
N_WBLK = 7
CONV_GROUPS = 2


def _conv_ln_swish(ext_ref, sh_ref, w_ref, bdw_ref, g_ref, b_ref, o_ref, tm):
    n_sh = tm + HALO - SUBLANES
    for r in range(1, SUBLANES):
        sh_ref[r - 1, :, :] = ext_ref[r:r + n_sh, :]
    base = HALO - (CONV_WIDTH - 1)
    taps = sorted(range(CONV_WIDTH), key=lambda j: ((base + j) % SUBLANES, j))
    for c in range(tm // (SUBLANES * CONV_GROUPS)):
        r0 = c * SUBLANES * CONV_GROUPS
        accs = [bdw_ref[...]] * CONV_GROUPS
        for j in taps:
            phase = (base + j) % SUBLANES
            wj = w_ref[j]
            for gi in range(CONV_GROUPS):
                a0 = r0 + gi * SUBLANES + base + j - phase
                if phase == 0:
                    src = ext_ref[a0:a0 + SUBLANES, :]
                else:
                    src = sh_ref[phase - 1, a0:a0 + SUBLANES, :]
                accs[gi] = accs[gi] + wj * src
        for gi in range(CONV_GROUPS):
            acc = accs[gi]
            mu = jnp.mean(acc, axis=-1, keepdims=True)
            d = acc - mu
            var = jnp.mean(d * d, axis=-1, keepdims=True)
            y = d * lax.rsqrt(var + EPS) * g_ref[...] + b_ref[...]
            rows = slice(r0 + gi * SUBLANES, r0 + (gi + 1) * SUBLANES)
            o_ref[rows, :] = (y * _sigmoid(y)).astype(o_ref.dtype)


def _inproj_kernel(*refs, tm, tiles_per_seq, n_cast):
    (x_ref, g_ref, w0, w1, w2, w3, w4, w5, w6, bu_ref, cw_ref, cb_ref, lg_ref, lb_ref) = refs[:14]
    cast_in = refs[14:14 + n_cast]
    lat_ref, sg_ref, cv_ref = refs[14 + n_cast:17 + n_cast]
    cast_out = refs[17 + n_cast:17 + 2 * n_cast]
    ext_ref, unew_ref, sh_ref = refs[17 + 2 * n_cast:]
    i = pl.program_id(0)

    @pl.when(i == 0)
    def _():
        ext_ref[...] = jnp.zeros(ext_ref.shape, F32)

    for src, dst in zip(cast_in, cast_out):
        dst[...] = src[...].astype(dst.dtype)

    _conv_ln_swish(ext_ref, sh_ref, cw_ref, cb_ref, lg_ref, lb_ref, cv_ref, tm)

    h = _rms(x_ref[...], g_ref[...]).astype(BF16)
    lat_ref[...] = jnp.dot(h, w0[...], preferred_element_type=F32)
    a = jnp.dot(h, w1[...], preferred_element_type=F32) + bu_ref[:, :CONV_CH]
    gt = jnp.dot(h, w2[...], preferred_element_type=F32) + bu_ref[:, CONV_CH:]
    unew_ref[...] = a * _sigmoid(gt)
    for c, w in enumerate((w3, w4, w5, w6)):
        acc = jnp.dot(h, w[...], preferred_element_type=F32)
        sg_ref[:, c * W_BLK:(c + 1) * W_BLK] = _sigmoid(acc).astype(sg_ref.dtype)

    @pl.when(i % tiles_per_seq == 0)
    def _():
        ext_ref[0:HALO, :] = jnp.zeros((HALO, CONV_CH), F32)

    @pl.when(i % tiles_per_seq != 0)
    def _():
        ext_ref[0:HALO, :] = ext_ref[tm:tm + HALO, :]

    ext_ref[HALO:, :] = unew_ref[...]


def in_proj_conv(x, g, w_all, b_u, w_dw, b_dw, ln_g, ln_b, cast_weights, seq, tm):
    m, d = x.shape
    n = m // tm
    cur = lambda w: pl.BlockSpec((tm, w), lambda i: (jnp.minimum(i, n - 1), 0))
    prev = lambda w: pl.BlockSpec((tm, w), lambda i: (jnp.maximum(i - 1, 0), 0))
    wblk = lambda c: pl.BlockSpec((d, W_BLK), lambda i: (0, c), pipeline_mode=pl.Buffered(1))
    cast_specs = [pl.BlockSpec((w.shape[0] // n, w.shape[1]), lambda i: (jnp.minimum(i, n - 1), 0))
                  for w in cast_weights]
    w8 = jnp.broadcast_to(w_dw[:, None, :], (CONV_WIDTH, SUBLANES, CONV_CH))
    vec = lambda a: jnp.broadcast_to(a.reshape(1, CONV_CH), (SUBLANES, CONV_CH))
    outs = pl.pallas_call(
        functools.partial(_inproj_kernel, tm=tm, tiles_per_seq=seq // tm,
                          n_cast=len(cast_weights)),
        grid=(n + 1,),
        in_specs=[cur(d), _resident((1, d))] + [wblk(c) for c in range(N_WBLK)]
                 + [_resident((1, 2 * CONV_CH)), _resident((CONV_WIDTH, SUBLANES, CONV_CH)),
                    _resident((SUBLANES, CONV_CH)), _resident((SUBLANES, CONV_CH)),
                    _resident((SUBLANES, CONV_CH))] + cast_specs,
        out_specs=[cur(W_BLK), cur(4 * W_BLK), prev(CONV_CH)] + cast_specs,
        out_shape=[jax.ShapeDtypeStruct((m, W_BLK), F32),
                   jax.ShapeDtypeStruct((m, 4 * W_BLK), BF16),
                   jax.ShapeDtypeStruct((m, CONV_CH), BF16)]
                  + [jax.ShapeDtypeStruct(w.shape, BF16) for w in cast_weights],
        scratch_shapes=[pltpu.VMEM((tm + HALO, CONV_CH), F32),
                        pltpu.VMEM((tm, CONV_CH), F32),
                        pltpu.VMEM((SUBLANES - 1, tm + HALO - SUBLANES, CONV_CH), F32)],
        compiler_params=_cparams(1),
        name="in_proj_conv",
    )(x, g.reshape(1, d), *([w_all] * N_WBLK), b_u.reshape(1, 2 * CONV_CH),
      w8, vec(b_dw), vec(ln_g), vec(ln_b), *cast_weights)
    return outs[0], outs[1], outs[2], outs[3:]


def _qkv_kernel(cq_ref, gq_ref, wqt_ref, c_ref, g_ref, kl_ref, tab_ref, wk_ref,
                wvt_ref, qt_ref, k_ref, vt_ref):
    cqn = _rms(cq_ref[...], gq_ref[...]).astype(BF16)
    c0 = (QK_NOPE + QK_ROPE) ** -0.5 * math.log2(math.e)
    tab_t = tab_ref[...].T * c0
    for hh in range(MLA_HEADS):
        r0 = hh * HEAD_PAD
        acc = lax.dot_general(wqt_ref[r0:r0 + HEAD_PAD, :], cqn, _NT,
                              preferred_element_type=F32)
        qt_ref[r0:r0 + QK_NOPE, :] = (acc[:QK_NOPE] * c0).astype(qt_ref.dtype)
        qt_ref[r0 + QK_NOPE:r0 + HEAD_PAD, :] = (acc[QK_NOPE:] * tab_t).astype(qt_ref.dtype)

    cn = _rms(c_ref[...], g_ref[...]).astype(BF16)
    prod = kl_ref[...] * tab_ref[...]
    krot2 = (prod + pltpu.roll(prod, QK_ROPE, 1)).astype(k_ref.dtype)
    for pair in range(MLA_HEADS // 2):
        acc = jnp.dot(cn, wk_ref[:, pair * 256:(pair + 1) * 256],
                      preferred_element_type=F32)
        for sub in range(2):
            hh = 2 * pair + sub
            k_ref[:, hh * HEAD_PAD:hh * HEAD_PAD + QK_NOPE] = (
                acc[:, sub * QK_NOPE:(sub + 1) * QK_NOPE].astype(k_ref.dtype))
            k_ref[:, hh * HEAD_PAD + QK_NOPE:(hh + 1) * HEAD_PAD] = krot2
    vt = lax.dot_general(wvt_ref[...], cn, _NT, preferred_element_type=F32).astype(vt_ref.dtype)
    ones = jnp.ones((BF16_ROWS, vt.shape[1]), vt_ref.dtype)
    for hh in range(MLA_HEADS):
        vt_ref[hh * VT_ROWS:hh * VT_ROWS + V_HEAD, :] = vt[hh * V_HEAD:(hh + 1) * V_HEAD, :]
        vt_ref[hh * VT_ROWS + V_HEAD:(hh + 1) * VT_ROWS, :] = ones


def proj_qkv(lat, g_q, w_qt, g_kv, w_k, w_vt, tab_k, tm):
    m = lat.shape[0]
    ckv_blk = Q_LORA // KV_LORA
    kl_blk = (Q_LORA + KV_LORA) // LANES
    nq, nk, nv = MLA_HEADS * HEAD_PAD, MLA_HEADS * HEAD_PAD, MLA_HEADS * VT_ROWS
    return pl.pallas_call(
        _qkv_kernel,
        grid=(m // tm,),
        in_specs=[pl.BlockSpec((tm, Q_LORA), lambda i: (i, 0)),
                  _resident((1, Q_LORA)),
                  _resident(w_qt.shape),
                  pl.BlockSpec((tm, KV_LORA), lambda i: (i, ckv_blk)),
                  _resident((1, KV_LORA)),
                  pl.BlockSpec((tm, LANES), lambda i: (i, kl_blk)),
                  pl.BlockSpec((tm, LANES), lambda i: (i, 0)),
                  _resident(w_k.shape), _resident(w_vt.shape)],
        out_specs=[pl.BlockSpec((nq, tm), lambda i: (0, i)),
                   pl.BlockSpec((tm, nk), lambda i: (i, 0)),
                   pl.BlockSpec((nv, tm), lambda i: (0, i))],
        out_shape=[jax.ShapeDtypeStruct((nq, m), BF16),
                   jax.ShapeDtypeStruct((m, nk), BF16),
                   jax.ShapeDtypeStruct((nv, m), BF16)],
        compiler_params=_cparams(1),
        name="proj_qkv",
    )(lat, g_q.reshape(1, Q_LORA), w_qt,
      lat, g_kv.reshape(1, KV_LORA), lat, tab_k, w_k, w_vt)


ATT_TQ = 256
ATT_HEADS = 2
ATT_LOOKAHEAD = 4


def _attn_kernel(qt_ref, k_ref, vt_ref, o_ref, *, seq):
    tq = ATT_TQ
    g_diag = tq // SUBLANES
    key_chunk = lax.broadcasted_iota(jnp.int32, (g_diag, SUBLANES, tq), 0) // (CHUNK // SUBLANES)
    qry_chunk = lax.broadcasted_iota(jnp.int32, (g_diag, SUBLANES, tq), 2) // CHUNK
    diag_mask = key_chunk <= qry_chunk

    def scores(hh, qi):
        kend = (qi + 1) * tq
        s = jnp.dot(k_ref[0:kend, hh * HEAD_PAD:(hh + 1) * HEAD_PAD],
                    qt_ref[hh * HEAD_PAD:(hh + 1) * HEAD_PAD, qi * tq:kend],
                    preferred_element_type=F32)
        return s.reshape(kend // SUBLANES, SUBLANES, tq)

    stages = [(hh, qi) for qi in range(seq // tq) for hh in range(ATT_HEADS)]
    pending = [scores(*st) for st in stages[:ATT_LOOKAHEAD]]
    for idx, (hh, qi) in enumerate(stages):
        q0, kend = qi * tq, (qi + 1) * tq
        s = pending.pop(0)
        if idx + ATT_LOOKAHEAD < len(stages):
            pending.append(scores(*stages[idx + ATT_LOOKAHEAD]))
        g_past = q0 // SUBLANES
        s_d = jnp.where(diag_mask, s[g_past:], NEG_INF)
        m8 = jnp.max(s_d, axis=0)
        if qi > 0:
            m8 = jnp.maximum(m8, jnp.max(s[:g_past], axis=0))
        m8 = jnp.broadcast_to(jnp.max(m8, axis=0, keepdims=True), (SUBLANES, tq))
        p = jnp.exp2(s_d - m8[None])
        if qi > 0:
            p = jnp.concatenate([jnp.exp2(s[:g_past] - m8[None]), p], axis=0)
        p = p.reshape(kend, tq).astype(BF16)
        ot = jnp.dot(vt_ref[hh * VT_ROWS:(hh + 1) * VT_ROWS, 0:kend], p,
                     preferred_element_type=F32)
        inv = 1.0 / ot[V_HEAD:V_HEAD + 1, :]
        o_ref[q0:kend, hh * V_HEAD:(hh + 1) * V_HEAD] = (
            (ot[:V_HEAD, :] * inv).T.astype(o_ref.dtype))


def attention(qt, k, vt, batch, seq):
    nh = ATT_HEADS
    return pl.pallas_call(
        functools.partial(_attn_kernel, seq=seq),
        grid=(batch, MLA_HEADS // nh),
        in_specs=[pl.BlockSpec((nh * HEAD_PAD, seq), lambda b, h: (h, b)),
                  pl.BlockSpec((seq, nh * HEAD_PAD), lambda b, h: (b, h)),
                  pl.BlockSpec((nh * VT_ROWS, seq), lambda b, h: (h, b))],
        out_specs=pl.BlockSpec((seq, nh * V_HEAD), lambda b, h: (b, h)),
        out_shape=jax.ShapeDtypeStruct((batch * seq, MLA_HEADS * V_HEAD), BF16),
        compiler_params=_cparams(2),
        name="mla_attention",
    )(qt, k, vt)


def _mix_kernel(v_ref, ya_ref, sga_ref, sgb_ref, x_ref, wpw_ref, bpw_ref, wo_ref, g_ref,
                x1_ref, h1_ref):
    yb = jnp.dot(v_ref[...], wpw_ref[...], preferred_element_type=F32) + bpw_ref[...]
    merged = (sga_ref[...].astype(F32) * ya_ref[...].astype(F32)
              + sgb_ref[...].astype(F32) * yb)
    x1 = x_ref[...] + jnp.dot(merged.astype(BF16), wo_ref[...], preferred_element_type=F32)
    x1_ref[...] = x1
    h1_ref[...] = _rms(x1, g_ref[...]).astype(h1_ref.dtype)


def mix(v, ya, sg, x, w_pw2, b_pw2, w_o, g, tm):
    m, d = x.shape
    row = lambda w: pl.BlockSpec((tm, w), lambda i: (i, 0))
    return pl.pallas_call(
        _mix_kernel,
        grid=(m // tm,),
        in_specs=[row(CONV_CH), row(d),
                  pl.BlockSpec((tm, d), lambda i: (i, 0)),
                  pl.BlockSpec((tm, d), lambda i: (i, 1)),
                  row(d),
                  _resident((CONV_CH, d)), _resident((1, d)), _resident((d, d)),
                  _resident((1, d))],
        out_specs=[row(d), row(d)],
        out_shape=[jax.ShapeDtypeStruct((m, d), F32), jax.ShapeDtypeStruct((m, d), BF16)],
        compiler_params=_cparams(1),
        name="mix_out_proj",
    )(v, ya, sg, sg, x, w_pw2, b_pw2.reshape(1, d), w_o, g.reshape(1, d))


def _memkv_kernel(m_ref, g_ref, w_ref, o_ref, mn_ref):
    @pl.when(pl.program_id(0) == 0)
    def _():
        mn_ref[...] = _rms(m_ref[...], g_ref[...]).astype(mn_ref.dtype)

    o_ref[...] = jnp.dot(mn_ref[...], w_ref[...], preferred_element_type=F32).astype(o_ref.dtype)


def mem_kv(mem, g, w, tn):
    m, k = mem.shape
    n = w.shape[1]
    return pl.pallas_call(
        _memkv_kernel,
        grid=(n // tn,),
        in_specs=[_resident((m, k)), _resident((1, k)),
                  pl.BlockSpec((k, tn), lambda j: (0, j))],
        out_specs=pl.BlockSpec((m, tn), lambda j: (0, j)),
        out_shape=jax.ShapeDtypeStruct((m, n), BF16),
        scratch_shapes=[pltpu.VMEM((m, k), BF16)],
        compiler_params=_cparams(1),
        name="mem_kv",
    )(mem, g.reshape(1, k), w)


def _cross_kernel(*refs, n_cast):
    h_ref, x_ref, kv_ref, wq_ref, wo_ref, g_ref = refs[:6]
    cast_in = refs[6:6 + n_cast]
    x2_ref, h2_ref = refs[6 + n_cast:8 + n_cast]
    cast_out = refs[8 + n_cast:8 + 2 * n_cast]
    o_scr = refs[8 + 2 * n_cast]

    for src, dst in zip(cast_in, cast_out):
        dst[...] = src[...].astype(dst.dtype)

    qm = jnp.dot(h_ref[...], wq_ref[...], preferred_element_type=F32)
    scale = MEM_HEAD_DIM ** -0.5

    def scores(hd):
        sl = slice(hd * MEM_HEAD_DIM, (hd + 1) * MEM_HEAD_DIM)
        return lax.dot_general(qm[:, sl].astype(BF16), kv_ref[:, sl], _NT,
                               preferred_element_type=F32) * scale

    nxt = scores(0)
    for hd in range(MEM_HEADS):
        sl = slice(hd * MEM_HEAD_DIM, (hd + 1) * MEM_HEAD_DIM)
        vsl = slice(D_MODEL + hd * MEM_HEAD_DIM, D_MODEL + (hd + 1) * MEM_HEAD_DIM)
        s = nxt
        if hd + 1 < MEM_HEADS:
            nxt = scores(hd + 1)
        m = jnp.max(s, axis=-1, keepdims=True)
        p = jnp.exp(s - m)
        l = jnp.sum(p, axis=-1, keepdims=True)
        o = jnp.dot(p.astype(BF16), kv_ref[:, vsl], preferred_element_type=F32)
        o_scr[:, sl] = (o * (1.0 / l)).astype(o_scr.dtype)
    x2 = x_ref[...] + jnp.dot(o_scr[...], wo_ref[...], preferred_element_type=F32)
    x2_ref[...] = x2
    h2_ref[...] = _rms(x2, g_ref[...]).astype(h2_ref.dtype)


def _cast_slice_rows(total_rows, n_steps):
    rows = -(-total_rows // n_steps)
    while total_rows % rows or rows % BF16_ROWS:
        rows += 1
    return rows


def cross_attention(h1, x1, kvm, w_q, w_o, g, cast_weights, batch, seq, mem_len, tm):
    d = D_MODEL
    nt = seq // tm
    row = pl.BlockSpec((tm, d), lambda b, s: (b * nt + s, 0))

    def cast_spec(w):
        rows = _cast_slice_rows(w.shape[0], batch * nt)
        last = w.shape[0] // rows - 1
        return pl.BlockSpec((rows, w.shape[1]), lambda b, s: (jnp.minimum(b * nt + s, last), 0))

    cast_specs = [cast_spec(w) for w in cast_weights]
    outs = pl.pallas_call(
        functools.partial(_cross_kernel, n_cast=len(cast_weights)),
        grid=(batch, nt),
        in_specs=[row, row,
                  pl.BlockSpec((mem_len, 2 * d), lambda b, s: (b, 0)),
                  _resident((d, d)), _resident((d, d)), _resident((1, d))] + cast_specs,
        out_specs=[row, row] + cast_specs,
        out_shape=[jax.ShapeDtypeStruct((batch * seq, d), F32),
                   jax.ShapeDtypeStruct((batch * seq, d), BF16)]
                  + [jax.ShapeDtypeStruct(w.shape, BF16) for w in cast_weights],
        scratch_shapes=[pltpu.VMEM((tm, d), BF16)],
        compiler_params=_cparams(2),
        name="cross_attention",
    )(h1, x1, kvm, w_q, w_o, g.reshape(1, d), *cast_weights)
    return outs[0], outs[1], outs[2:]


FFN_TN = 512


def _ffn_up_kernel(h_ref, wg_ref, wu_ref, o_ref):
    h = h_ref[...]
    gt = jnp.dot(h, wg_ref[...], preferred_element_type=F32)
    up = jnp.dot(h, wu_ref[...], preferred_element_type=F32)
    o_ref[...] = (gt * _sigmoid(gt) * up).astype(o_ref.dtype)


def ffn_up(h, w_gate, w_up, tm):
    m, k = h.shape
    n = w_gate.shape[1]
    wspec = pl.BlockSpec((k, FFN_TN), lambda j, i: (0, j))
    return pl.pallas_call(
        _ffn_up_kernel,
        grid=(n // FFN_TN, m // tm),
        in_specs=[pl.BlockSpec((tm, k), lambda j, i: (i, 0)), wspec, wspec],
        out_specs=pl.BlockSpec((tm, FFN_TN), lambda j, i: (i, j)),
        out_shape=jax.ShapeDtypeStruct((m, n), BF16),
        compiler_params=_cparams(2),
        name="ffn_up",
    )(h, w_gate, w_up)


def _ffn_down_kernel(a_ref, w_ref, x_ref, g_ref, o_ref):
    x3 = x_ref[...] + jnp.dot(a_ref[...], w_ref[...], preferred_element_type=F32)
    o_ref[...] = _rms(x3, g_ref[...])


def ffn_down(a, w, x, g, tm):
    m, kdim = a.shape
    d = w.shape[1]
    return pl.pallas_call(
        _ffn_down_kernel,
        grid=(m // tm,),
        in_specs=[pl.BlockSpec((tm, kdim), lambda i: (i, 0)),
                  _resident((kdim, d)),
                  pl.BlockSpec((tm, d), lambda i: (i, 0)),
                  _resident((1, d))],
        out_specs=pl.BlockSpec((tm, d), lambda i: (i, 0)),
        out_shape=jax.ShapeDtypeStruct((m, d), F32),
        compiler_params=_cparams(1),
        name="ffn_down",
    )(a, w, x, g.reshape(1, d))


def _swap_halves(w):
    half = w.shape[-1] // 2
    return jnp.concatenate([w[..., half:], w[..., :half]], axis=-1)


PACK_TAIL = 256


def _pack_w_kernel(tail_ref, blk_ref, o_ref):
    j = pl.program_id(0)
    o_kr = Q_LORA + KV_LORA
    half = QK_ROPE // 2
    lat_src = o_kr + QK_ROPE

    @pl.when(j == 0)
    def _():
        b = blk_ref[...]
        rows = jnp.concatenate(
            [b[0:lat_src], b[o_kr + half:lat_src], b[o_kr:o_kr + half],
             jnp.zeros((W_BLK - LAT_W, b.shape[1]), b.dtype)], axis=0)
        o_ref[...] = rows.T.astype(o_ref.dtype)

    @pl.when(j > 0)
    def _():
        n_tail = W_BLK - lat_src
        rows = jnp.concatenate(
            [tail_ref[PACK_TAIL - n_tail:PACK_TAIL, :], blk_ref[0:lat_src, :]], axis=0)
        o_ref[...] = rows.T.astype(o_ref.dtype)


def pack_w_in(w_t):
    n_in, d = w_t.shape
    tails_per_blk = W_BLK // PACK_TAIL
    return pl.pallas_call(
        _pack_w_kernel,
        grid=(N_WBLK,),
        in_specs=[pl.BlockSpec((PACK_TAIL, d),
                               lambda j: (jnp.maximum(j, 1) * tails_per_blk - 1, 0)),
                  pl.BlockSpec((W_BLK, d), lambda j: (j, 0))],
        out_specs=pl.BlockSpec((d, W_BLK), lambda j: (0, j)),
        out_shape=jax.ShapeDtypeStruct((d, N_WBLK * W_BLK), BF16),
        compiler_params=_cparams(1),
        name="pack_w_in",
    )(w_t, w_t)


def _rope_table(positions):
    inv_freq = 1.0 / (ROPE_THETA ** (jnp.arange(0, QK_ROPE, 2, dtype=F32) / QK_ROPE))
    ang = positions.astype(F32).reshape(-1, 1) * inv_freq
    cos, sin = jnp.cos(ang), jnp.sin(ang)
    return jnp.concatenate([cos, cos, -sin, sin], axis=-1)


def kernel(x, mem, positions, norm_mix, w_in, b_conv_in, norm_cq, w_uq, norm_ckv, w_ukv, w_dw, b_dw, ln_conv_g, ln_conv_b, w_pw2, b_pw2, w_o, norm_cross, norm_mem, w_q_mem, w_kv_mem, w_o_mem, norm_ffn, w_gate, w_up, w_down, norm_final):
    batch, seq, d = x.shape
    mem_len = mem.shape[1]
    t = batch * seq
    xf = x.reshape(t, d)
    tab_k = _rope_table(positions)
    l = 0

    w_all = pack_w_in(w_in[l].T)
    wq3 = w_uq[l].reshape(Q_LORA, MLA_HEADS, QK_NOPE + QK_ROPE)
    w_qt = jnp.concatenate(
        [wq3, _swap_halves(wq3[..., QK_NOPE:])], axis=-1).reshape(Q_LORA, -1).T.astype(BF16)
    wkv3 = w_ukv[l].reshape(KV_LORA, MLA_HEADS, QK_NOPE + V_HEAD)
    w_k = wkv3[..., :QK_NOPE].reshape(KV_LORA, -1).astype(BF16)
    w_vt = wkv3[..., QK_NOPE:].reshape(KV_LORA, -1).T.astype(BF16)

    later_weights = (w_pw2[l], w_o[l], w_q_mem[l], w_o_mem[l], w_kv_mem[l])
    lat, sg, cv, (w_pw2_b, w_o_b, w_qm_b, w_om_b, w_kvm_b) = in_proj_conv(
        xf, norm_mix[l], w_all, b_conv_in[l], w_dw[l], b_dw[l], ln_conv_g[l], ln_conv_b[l],
        later_weights, seq, tm=256)
    qt, k, vt = proj_qkv(lat, norm_cq[l], w_qt, norm_ckv[l], w_k, w_vt, tab_k, tm=512)
    y_a = attention(qt, k, vt, batch, seq)
    x1, h1 = mix(cv, y_a, sg, xf, w_pw2_b, b_pw2[l], w_o_b, norm_cross[l], tm=512)

    kvm = mem_kv(mem.reshape(batch * mem_len, d), norm_mem[l], w_kvm_b, tn=1024)
    x2, h2, (w_gate_b, w_up_b, w_down_b) = cross_attention(
        h1, x1, kvm, w_qm_b, w_om_b, norm_ffn[l], (w_gate[l], w_up[l], w_down[l]),
        batch, seq, mem_len, tm=256)

    a = ffn_up(h2, w_gate_b, w_up_b, tm=1024)
    out = ffn_down(a, w_down_b, x2, norm_final, tm=256)
    return out.reshape(batch, seq, d)
```

```python
import functools
import math

import jax
import jax.numpy as jnp
from jax import lax
from jax.experimental import pallas as pl
from jax.experimental.pallas import tpu as pltpu

F32 = jnp.float32
BF16 = jnp.bfloat16

D_MODEL = 2048
CHUNK = 64
MLA_HEADS = 16
QK_NOPE = 128
QK_ROPE = 64
V_HEAD = 128
Q_LORA = 512
KV_LORA = 256
ROPE_THETA = 10000.0
CONV_CH = 1024
CONV_WIDTH = 31
MEM_HEADS = 4
MEM_HEAD_DIM = D_MODEL // MEM_HEADS
EPS = 1e-6
NEG_INF = -1e30

LANES = 128
SUBLANES = 8
HEAD_PAD = 256
BF16_ROWS = 16
VT_ROWS = V_HEAD + BF16_ROWS
LAT_W = Q_LORA + KV_LORA + 2 * QK_ROPE
HALO = 32
VMEM_LIMIT = 56 * 1024 * 1024

_NT = (((1,), (1,)), ((), ()))


def _cparams(n_axes):
    return pltpu.CompilerParams(
        dimension_semantics=("arbitrary",) * n_axes,
        vmem_limit_bytes=VMEM_LIMIT)


def _sigmoid(x):
    return 1.0 / (1.0 + jnp.exp(-x))


def _rms(x, g):
    ms = jnp.mean(x * x, axis=-1, keepdims=True)
    return x * lax.rsqrt(ms + EPS) * g


def _resident(shape):
    zeros = (0,) * len(shape)
    return pl.BlockSpec(shape, lambda *_: zeros, pipeline_mode=pl.Buffered(1))


W_BLK = 1024
N_WBLK = 7
CONV_GROUPS = 2


def _conv_ln_swish(ext_ref, sh_ref, w_ref, bdw_ref, g_ref, b_ref, o_ref, tm):
    n_sh = tm + HALO - SUBLANES
    for r in range(1, SUBLANES):
        sh_ref[r - 1, :, :] = ext_ref[r:r + n_sh, :]
    base = HALO - (CONV_WIDTH - 1)
    taps = sorted(range(CONV_WIDTH), key=lambda j: ((base + j) % SUBLANES, j))
    for c in range(tm // (SUBLANES * CONV_GROUPS)):
        r0 = c * SUBLANES * CONV_GROUPS
        accs = [bdw_ref[...]] * CONV_GROUPS
        for j in taps:
            phase = (base + j) % SUBLANES
            wj = w_ref[j]
            for gi in range(CONV_GROUPS):
                a0 = r0 + gi * SUBLANES + base + j - phase
                if phase == 0:
                    src = ext_ref[a0:a0 + SUBLANES, :]
                else:
                    src = sh_ref[phase - 1, a0:a0 + SUBLANES, :]
                accs[gi] = accs[gi] + wj * src
        for gi in range(CONV_GROUPS):
            acc = accs[gi]
            mu = jnp.mean(acc, axis=-1, keepdims=True)
            d = acc - mu
            var = jnp.mean(d * d, axis=-1, keepdims=True)
            y = d * lax.rsqrt(var + EPS) * g_ref[...] + b_ref[...]
            rows = slice(r0 + gi * SUBLANES, r0 + (gi + 1) * SUBLANES)
            o_ref[rows, :] = (y * _sigmoid(y)).astype(o_ref.dtype)


def _inproj_kernel(*refs, tm, tiles_per_seq, n_cast):
    (x_ref, g_ref, w0, w1, w2, w3, w4, w5, w6, bu_ref, cw_ref, cb_ref, lg_ref, lb_ref) = refs[:14]
    cast_in = refs[14:14 + n_cast]
    lat_ref, sg_ref, cv_ref = refs[14 + n_cast:17 + n_cast]
    cast_out = refs[17 + n_cast:17 + 2 * n_cast]
    ext_ref, unew_ref, sh_ref = refs[17 + 2 * n_cast:]
    i = pl.program_id(0)

    @pl.when(i == 0)
    def _():
        ext_ref[...] = jnp.zeros(ext_ref.shape, F32)

    for src, dst in zip(cast_in, cast_out):
        dst[...] = src[...].astype(dst.dtype)

    _conv_ln_swish(ext_ref, sh_ref, cw_ref, cb_ref, lg_ref, lb_ref, cv_ref, tm)

    h = _rms(x_ref[...], g_ref[...]).astype(BF16)
    lat_ref[...] = jnp.dot(h, w0[...], preferred_element_type=F32)
    a = jnp.dot(h, w1[...], preferred_element_type=F32) + bu_ref[:, :CONV_CH]
    gt = jnp.dot(h, w2[...], preferred_element_type=F32) + bu_ref[:, CONV_CH:]
    unew_ref[...] = a * _sigmoid(gt)
    for c, w in enumerate((w3, w4, w5, w6)):
        acc = jnp.dot(h, w[...], preferred_element_type=F32)
        sg_ref[:, c * W_BLK:(c + 1) * W_BLK] = _sigmoid(acc).astype(sg_ref.dtype)

    @pl.when(i % tiles_per_seq == 0)
    def _():
        ext_ref[0:HALO, :] = jnp.zeros((HALO, CONV_CH), F32)

    @pl.when(i % tiles_per_seq != 0)
    def _():
        ext_ref[0:HALO, :] = ext_ref[tm:tm + HALO, :]

    ext_ref[HALO:, :] = unew_ref[...]


def in_proj_conv(x, g, w_all, b_u, w_dw, b_dw, ln_g, ln_b, cast_weights, seq, tm):
    m, d = x.shape
    n = m // tm
    cur = lambda w: pl.BlockSpec((tm, w), lambda i: (jnp.minimum(i, n - 1), 0))
    prev = lambda w: pl.BlockSpec((tm, w), lambda i: (jnp.maximum(i - 1, 0), 0))
    wblk = lambda c: pl.BlockSpec((d, W_BLK), lambda i: (0, c), pipeline_mode=pl.Buffered(1))
    cast_specs = [pl.BlockSpec((w.shape[0] // n, w.shape[1]), lambda i: (jnp.minimum(i, n - 1), 0))
                  for w in cast_weights]
    w8 = jnp.broadcast_to(w_dw[:, None, :], (CONV_WIDTH, SUBLANES, CONV_CH))
    vec = lambda a: jnp.broadcast_to(a.reshape(1, CONV_CH), (SUBLANES, CONV_CH))
    outs = pl.pallas_call(
        functools.partial(_inproj_kernel, tm=tm, tiles_per_seq=seq // tm,
                          n_cast=len(cast_weights)),
        grid=(n + 1,),
        in_specs=[cur(d), _resident((1, d))] + [wblk(c) for c in range(N_WBLK)]
                 + [_resident((1, 2 * CONV_CH)), _resident((CONV_WIDTH, SUBLANES, CONV_CH)),
                    _resident((SUBLANES, CONV_CH)), _resident((SUBLANES, CONV_CH)),
                    _resident((SUBLANES, CONV_CH))] + cast_specs,
        out_specs=[cur(W_BLK), cur(4 * W_BLK), prev(CONV_CH)] + cast_specs,
        out_shape=[jax.ShapeDtypeStruct((m, W_BLK), F32),
                   jax.ShapeDtypeStruct((m, 4 * W_BLK), BF16),
                   jax.ShapeDtypeStruct((m, CONV_CH), BF16)]
                  + [jax.ShapeDtypeStruct(w.shape, BF16) for w in cast_weights],
        scratch_shapes=[pltpu.VMEM((tm + HALO, CONV_CH), F32),
                        pltpu.VMEM((tm, CONV_CH), F32),
                        pltpu.VMEM((SUBLANES - 1, tm + HALO - SUBLANES, CONV_CH), F32)],
        compiler_params=_cparams(1),
        name="in_proj_conv",
    )(x, g.reshape(1, d), *([w_all] * N_WBLK), b_u.reshape(1, 2 * CONV_CH),
      w8, vec(b_dw), vec(ln_g), vec(ln_b), *cast_weights)
    return outs[0], outs[1], outs[2], outs[3:]


def _qkv_kernel(cq_ref, gq_ref, wqt_ref, c_ref, g_ref, kl_ref, tab_ref, wk_ref,
                wvt_ref, qt_ref, k_ref, vt_ref):
    cqn = _rms(cq_ref[...], gq_ref[...]).astype(BF16)
    c0 = (QK_NOPE + QK_ROPE) ** -0.5 * math.log2(math.e)
    tab_t = tab_ref[...].T * c0
    for hh in range(MLA_HEADS):
        r0 = hh * HEAD_PAD
        acc = lax.dot_general(wqt_ref[r0:r0 + HEAD_PAD, :], cqn, _NT,
                              preferred_element_type=F32)
        qt_ref[r0:r0 + QK_NOPE, :] = (acc[:QK_NOPE] * c0).astype(qt_ref.dtype)
        qt_ref[r0 + QK_NOPE:r0 + HEAD_PAD, :] = (acc[QK_NOPE:] * tab_t).astype(qt_ref.dtype)

    cn = _rms(c_ref[...], g_ref[...]).astype(BF16)
    prod = kl_ref[...] * tab_ref[...]
    krot2 = (prod + pltpu.roll(prod, QK_ROPE, 1)).astype(k_ref.dtype)
    for pair in range(MLA_HEADS // 2):
        acc = jnp.dot(cn, wk_ref[:, pair * 256:(pair + 1) * 256],
                      preferred_element_type=F32)
        for sub in range(2):
            hh = 2 * pair + sub
            k_ref[:, hh * HEAD_PAD:hh * HEAD_PAD + QK_NOPE] = (
                acc[:, sub * QK_NOPE:(sub + 1) * QK_NOPE].astype(k_ref.dtype))
            k_ref[:, hh * HEAD_PAD + QK_NOPE:(hh + 1) * HEAD_PAD] = krot2
    vt = lax.dot_general(wvt_ref[...], cn, _NT, preferred_element_type=F32).astype(vt_ref.dtype)
    ones = jnp.ones((BF16_ROWS, vt.shape[1]), vt_ref.dtype)
    for hh in range(MLA_HEADS):
        vt_ref[hh * VT_ROWS:hh * VT_ROWS + V_HEAD, :] = vt[hh * V_HEAD:(hh + 1) * V_HEAD, :]
        vt_ref[hh * VT_ROWS + V_HEAD:(hh + 1) * VT_ROWS, :] = ones


def proj_qkv(lat, g_q, w_qt, g_kv, w_k, w_vt, tab_k, tm):
    m = lat.shape[0]
    ckv_blk = Q_LORA // KV_LORA
    kl_blk = (Q_LORA + KV_LORA) // LANES
    nq, nk, nv = MLA_HEADS * HEAD_PAD, MLA_HEADS * HEAD_PAD, MLA_HEADS * VT_ROWS
    return pl.pallas_call(
        _qkv_kernel,
        grid=(m // tm,),
        in_specs=[pl.BlockSpec((tm, Q_LORA), lambda i: (i, 0)),
                  _resident((1, Q_LORA)),
                  _resident(w_qt.shape),
                  pl.BlockSpec((tm, KV_LORA), lambda i: (i, ckv_blk)),
                  _resident((1, KV_LORA)),
                  pl.BlockSpec((tm, LANES), lambda i: (i, kl_blk)),
                  pl.BlockSpec((tm, LANES), lambda i: (i, 0)),
                  _resident(w_k.shape), _resident(w_vt.shape)],
        out_specs=[pl.BlockSpec((nq, tm), lambda i: (0, i)),
                   pl.BlockSpec((tm, nk), lambda i: (i, 0)),
                   pl.BlockSpec((nv, tm), lambda i: (0, i))],
        out_shape=[jax.ShapeDtypeStruct((nq, m), BF16),
                   jax.ShapeDtypeStruct((m, nk), BF16),
                   jax.ShapeDtypeStruct((nv, m), BF16)],
        compiler_params=_cparams(1),
        name="proj_qkv",
    )(lat, g_q.reshape(1, Q_LORA), w_qt,
      lat, g_kv.reshape(1, KV_LORA), lat, tab_k, w_k, w_vt)


ATT_TQ = 256
ATT_HEADS = 2
ATT_LOOKAHEAD = 4


def _attn_kernel(qt_ref, k_ref, vt_ref, o_ref, *, seq):
    tq = ATT_TQ
    g_diag = tq // SUBLANES
    key_chunk = lax.broadcasted_iota(jnp.int32, (g_diag, SUBLANES, tq), 0) // (CHUNK // SUBLANES)
    qry_chunk = lax.broadcasted_iota(jnp.int32, (g_diag, SUBLANES, tq), 2) // CHUNK
    diag_mask = key_chunk <= qry_chunk

    def scores(hh, qi, kb):
        s = jnp.dot(k_ref[kb * tq:(kb + 1) * tq, hh * HEAD_PAD:(hh + 1) * HEAD_PAD],
                    qt_ref[hh * HEAD_PAD:(hh + 1) * HEAD_PAD, qi * tq:(qi + 1) * tq],
                    preferred_element_type=F32)
        return s.reshape(g_diag, SUBLANES, tq)

    stages = [(hh, qi, kb) for qi in range(seq // tq) for kb in range(qi + 1)
              for hh in range(ATT_HEADS)]
    pending = [scores(*st) for st in stages[:ATT_LOOKAHEAD]]
    run_max, run_out = {}, {}
    for idx, (hh, qi, kb) in enumerate(stages):
        s = pending.pop(0)
        if idx + ATT_LOOKAHEAD < len(stages):
            pending.append(scores(*stages[idx + ATT_LOOKAHEAD]))
        if kb == qi:
            s = jnp.where(diag_mask, s, NEG_INF)
        m8 = jnp.max(s, axis=0)
        m8 = jnp.broadcast_to(jnp.max(m8, axis=0, keepdims=True), (SUBLANES, tq))
        if kb > 0:
            m_old = run_max[hh, qi]
            m8 = jnp.maximum(m8, m_old)
        p = jnp.exp2(s - m8[None]).reshape(tq, tq).astype(BF16)
        ot = jnp.dot(vt_ref[hh * VT_ROWS:(hh + 1) * VT_ROWS, kb * tq:(kb + 1) * tq], p,
                     preferred_element_type=F32)
        if kb > 0:
            alpha = jnp.exp2(m_old - m8)[0:1, :]
            ot = ot + run_out[hh, qi] * alpha
        run_max[hh, qi], run_out[hh, qi] = m8, ot
        if kb == qi:
            inv = 1.0 / ot[V_HEAD:V_HEAD + 1, :]
            o_ref[qi * tq:(qi + 1) * tq, hh * V_HEAD:(hh + 1) * V_HEAD] = (
                (ot[:V_HEAD, :] * inv).T.astype(o_ref.dtype))


def attention(qt, k, vt, batch, seq):
    nh = ATT_HEADS
    return pl.pallas_call(
        functools.partial(_attn_kernel, seq=seq),
        grid=(batch, MLA_HEADS // nh),
        in_specs=[pl.BlockSpec((nh * HEAD_PAD, seq), lambda b, h: (h, b)),
                  pl.BlockSpec((seq, nh * HEAD_PAD), lambda b, h: (b, h)),
                  pl.BlockSpec((nh * VT_ROWS, seq), lambda b, h: (h, b))],
        out_specs=pl.BlockSpec((seq, nh * V_HEAD), lambda b, h: (b, h)),
        out_shape=jax.ShapeDtypeStruct((batch * seq, MLA_HEADS * V_HEAD), BF16),
        compiler_params=_cparams(2),
        name="mla_attention",
    )(qt, k, vt)


def _mix_kernel(v_ref, ya_ref, sga_ref, sgb_ref, x_ref, wpw_ref, bpw_ref, wo_ref, g_ref,
                x1_ref, h1_ref):
    yb = jnp.dot(v_ref[...], wpw_ref[...], preferred_element_type=F32) + bpw_ref[...]
    merged = (sga_ref[...].astype(F32) * ya_ref[...].astype(F32)
              + sgb_ref[...].astype(F32) * yb)
    x1 = x_ref[...] + jnp.dot(merged.astype(BF16), wo_ref[...], preferred_element_type=F32)
    x1_ref[...] = x1
    h1_ref[...] = _rms(x1, g_ref[...]).astype(h1_ref.dtype)


def mix(v, ya, sg, x, w_pw2, b_pw2, w_o, g, tm):
    m, d = x.shape
    row = lambda w: pl.BlockSpec((tm, w), lambda i: (i, 0))
    return pl.pallas_call(
        _mix_kernel,
        grid=(m // tm,),
        in_specs=[row(CONV_CH), row(d),
                  pl.BlockSpec((tm, d), lambda i: (i, 0)),
                  pl.BlockSpec((tm, d), lambda i: (i, 1)),
                  row(d),
                  _resident((CONV_CH, d)), _resident((1, d)), _resident((d, d)),
                  _resident((1, d))],
        out_specs=[row(d), row(d)],
        out_shape=[jax.ShapeDtypeStruct((m, d), F32), jax.ShapeDtypeStruct((m, d), BF16)],
        compiler_params=_cparams(1),
        name="mix_out_proj",
    )(v, ya, sg, sg, x, w_pw2, b_pw2.reshape(1, d), w_o, g.reshape(1, d))


def _memkv_kernel(m_ref, g_ref, w_ref, o_ref, mn_ref):
    @pl.when(pl.program_id(0) == 0)
    def _():
        mn_ref[...] = _rms(m_ref[...], g_ref[...]).astype(mn_ref.dtype)

    o_ref[...] = jnp.dot(mn_ref[...], w_ref[...], preferred_element_type=F32).astype(o_ref.dtype)


def mem_kv(mem, g, w, tn):
    m, k = mem.shape
    n = w.shape[1]
    return pl.pallas_call(
        _memkv_kernel,
        grid=(n // tn,),
        in_specs=[_resident((m, k)), _resident((1, k)),
                  pl.BlockSpec((k, tn), lambda j: (0, j))],
        out_specs=pl.BlockSpec((m, tn), lambda j: (0, j)),
        out_shape=jax.ShapeDtypeStruct((m, n), BF16),
        scratch_shapes=[pltpu.VMEM((m, k), BF16)],
        compiler_params=_cparams(1),
        name="mem_kv",
    )(mem, g.reshape(1, k), w)


def _cross_kernel(*refs, n_cast):
    h_ref, x_ref, kv_ref, wq_ref, wo_ref, g_ref = refs[:6]
    cast_in = refs[6:6 + n_cast]
    x2_ref, h2_ref = refs[6 + n_cast:8 + n_cast]
    cast_out = refs[8 + n_cast:8 + 2 * n_cast]
    o_scr = refs[8 + 2 * n_cast]

    for src, dst in zip(cast_in, cast_out):
        dst[...] = src[...].astype(dst.dtype)

    qm = jnp.dot(h_ref[...], wq_ref[...], preferred_element_type=F32)
    scale = MEM_HEAD_DIM ** -0.5

    def scores(hd):
        sl = slice(hd * MEM_HEAD_DIM, (hd + 1) * MEM_HEAD_DIM)
        return lax.dot_general(qm[:, sl].astype(BF16), kv_ref[:, sl], _NT,
                               preferred_element_type=F32) * scale

    nxt = scores(0)
    for hd in range(MEM_HEADS):
        sl = slice(hd * MEM_HEAD_DIM, (hd + 1) * MEM_HEAD_DIM)
        vsl = slice(D_MODEL + hd * MEM_HEAD_DIM, D_MODEL + (hd + 1) * MEM_HEAD_DIM)
        s = nxt
        if hd + 1 < MEM_HEADS:
            nxt = scores(hd + 1)
        m = jnp.max(s, axis=-1, keepdims=True)
        p = jnp.exp(s - m)
        l = jnp.sum(p, axis=-1, keepdims=True)
        o = jnp.dot(p.astype(BF16), kv_ref[:, vsl], preferred_element_type=F32)
        o_scr[:, sl] = (o * (1.0 / l)).astype(o_scr.dtype)
    x2 = x_ref[...] + jnp.dot(o_scr[...], wo_ref[...], preferred_element_type=F32)
    x2_ref[...] = x2
    h2_ref[...] = _rms(x2, g_ref[...]).astype(h2_ref.dtype)


def _cast_slice_rows(total_rows, n_steps):
    rows = -(-total_rows // n_steps)
    while total_rows % rows or rows % BF16_ROWS:
        rows += 1
    return rows


def cross_attention(h1, x1, kvm, w_q, w_o, g, cast_weights, batch, seq, mem_len, tm):
    d = D_MODEL
    nt = seq // tm
    row = pl.BlockSpec((tm, d), lambda b, s: (b * nt + s, 0))

    def cast_spec(w):
        rows = _cast_slice_rows(w.shape[0], batch * nt)
        last = w.shape[0] // rows - 1
        return pl.BlockSpec((rows, w.shape[1]), lambda b, s: (jnp.minimum(b * nt + s, last), 0))

    cast_specs = [cast_spec(w) for w in cast_weights]
    outs = pl.pallas_call(
        functools.partial(_cross_kernel, n_cast=len(cast_weights)),
        grid=(batch, nt),
        in_specs=[row, row,
                  pl.BlockSpec((mem_len, 2 * d), lambda b, s: (b, 0)),
                  _resident((d, d)), _resident((d, d)), _resident((1, d))] + cast_specs,
        out_specs=[row, row] + cast_specs,
        out_shape=[jax.ShapeDtypeStruct((batch * seq, d), F32),
                   jax.ShapeDtypeStruct((batch * seq, d), BF16)]
                  + [jax.ShapeDtypeStruct(w.shape, BF16) for w in cast_weights],
        scratch_shapes=[pltpu.VMEM((tm, d), BF16)],
        compiler_params=_cparams(2),
        name="cross_attention",
    )(h1, x1, kvm, w_q, w_o, g.reshape(1, d), *cast_weights)
    return outs[0], outs[1], outs[2:]


FFN_TN = 512


def _ffn_up_kernel(h_ref, wg_ref, wu_ref, o_ref):
    h = h_ref[...]
    gt = jnp.dot(h, wg_ref[...], preferred_element_type=F32)
    up = jnp.dot(h, wu_ref[...], preferred_element_type=F32)
    o_ref[...] = (gt * _sigmoid(gt) * up).astype(o_ref.dtype)


def ffn_up(h, w_gate, w_up, tm):
    m, k = h.shape
    n = w_gate.shape[1]
    wspec = pl.BlockSpec((k, FFN_TN), lambda j, i: (0, j))
    return pl.pallas_call(
        _ffn_up_kernel,
        grid=(n // FFN_TN, m // tm),
        in_specs=[pl.BlockSpec((tm, k), lambda j, i: (i, 0)), wspec, wspec],
        out_specs=pl.BlockSpec((tm, FFN_TN), lambda j, i: (i, j)),
        out_shape=jax.ShapeDtypeStruct((m, n), BF16),
        compiler_params=_cparams(2),
        name="ffn_up",
    )(h, w_gate, w_up)


def _ffn_down_kernel(a_ref, w_ref, x_ref, g_ref, o_ref):
    x3 = x_ref[...] + jnp.dot(a_ref[...], w_ref[...], preferred_element_type=F32)
    o_ref[...] = _rms(x3, g_ref[...])


def ffn_down(a, w, x, g, tm):
    m, kdim = a.shape
    d = w.shape[1]
    return pl.pallas_call(
        _ffn_down_kernel,
        grid=(m // tm,),
        in_specs=[pl.BlockSpec((tm, kdim), lambda i: (i, 0)),
                  _resident((kdim, d)),
                  pl.BlockSpec((tm, d), lambda i: (i, 0)),
                  _resident((1, d))],
        out_specs=pl.BlockSpec((tm, d), lambda i: (i, 0)),
        out_shape=jax.ShapeDtypeStruct((m, d), F32),
        compiler_params=_cparams(1),
        name="ffn_down",
    )(a, w, x, g.reshape(1, d))


def _swap_halves(w):
    half = w.shape[-1] // 2
    return jnp.concatenate([w[..., half:], w[..., :half]], axis=-1)


PACK_TAIL = 256


def _pack_w_kernel(tail_ref, blk_ref, o_ref):
    j = pl.program_id(0)
    o_kr = Q_LORA + KV_LORA
    half = QK_ROPE // 2
    lat_src = o_kr + QK_ROPE

    @pl.when(j == 0)
    def _():
        b = blk_ref[...]
        rows = jnp.concatenate(
            [b[0:lat_src], b[o_kr + half:lat_src], b[o_kr:o_kr + half],
             jnp.zeros((W_BLK - LAT_W, b.shape[1]), b.dtype)], axis=0)
        o_ref[...] = rows.T.astype(o_ref.dtype)

    @pl.when(j > 0)
    def _():
        n_tail = W_BLK - lat_src
        rows = jnp.concatenate(
            [tail_ref[PACK_TAIL - n_tail:PACK_TAIL, :], blk_ref[0:lat_src, :]], axis=0)
        o_ref[...] = rows.T.astype(o_ref.dtype)


def pack_w_in(w_t):
    n_in, d = w_t.shape
    tails_per_blk = W_BLK // PACK_TAIL
    return pl.pallas_call(
        _pack_w_kernel,
        grid=(N_WBLK,),
        in_specs=[pl.BlockSpec((PACK_TAIL, d),
                               lambda j: (jnp.maximum(j, 1) * tails_per_blk - 1, 0)),
                  pl.BlockSpec((W_BLK, d), lambda j: (j, 0))],
        out_specs=pl.BlockSpec((d, W_BLK), lambda j: (0, j)),
        out_shape=jax.ShapeDtypeStruct((d, N_WBLK * W_BLK), BF16),
        compiler_params=_cparams(1),
        name="pack_w_in",
    )(w_t, w_t)


def _rope_table(positions):
    inv_freq = 1.0 / (ROPE_THETA ** (jnp.arange(0, QK_ROPE, 2, dtype=F32) / QK_ROPE))
    ang = positions.astype(F32).reshape(-1, 1) * inv_freq
    cos, sin = jnp.cos(ang), jnp.sin(ang)
    return jnp.concatenate([cos, cos, -sin, sin], axis=-1)


def kernel(x, mem, positions, norm_mix, w_in, b_conv_in, norm_cq, w_uq, norm_ckv, w_ukv, w_dw, b_dw, ln_conv_g, ln_conv_b, w_pw2, b_pw2, w_o, norm_cross, norm_mem, w_q_mem, w_kv_mem, w_o_mem, norm_ffn, w_gate, w_up, w_down, norm_final):
    batch, seq, d = x.shape
    mem_len = mem.shape[1]
    t = batch * seq
    xf = x.reshape(t, d)
    tab_k = _rope_table(positions)
    l = 0

    w_all = pack_w_in(w_in[l].T)
    wq3 = w_uq[l].reshape(Q_LORA, MLA_HEADS, QK_NOPE + QK_ROPE)
    w_qt = jnp.concatenate(
        [wq3, _swap_halves(wq3[..., QK_NOPE:])], axis=-1).reshape(Q_LORA, -1).T.astype(BF16)
    wkv3 = w_ukv[l].reshape(KV_LORA, MLA_HEADS, QK_NOPE + V_HEAD)
    w_k = wkv3[..., :QK_NOPE].reshape(KV_LORA, -1).astype(BF16)
    w_vt = wkv3[..., QK_NOPE:].reshape(KV_LORA, -1).T.astype(BF16)

    later_weights = (w_pw2[l], w_o[l], w_q_mem[l], w_o_mem[l], w_kv_mem[l])
    lat, sg, cv, (w_pw2_b, w_o_b, w_qm_b, w_om_b, w_kvm_b) = in_proj_conv(
        xf, norm_mix[l], w_all, b_conv_in[l], w_dw[l], b_dw[l], ln_conv_g[l], ln_conv_b[l],
        later_weights, seq, tm=256)
    qt, k, vt = proj_qkv(lat, norm_cq[l], w_qt, norm_ckv[l], w_k, w_vt, tab_k, tm=512)
    y_a = attention(qt, k, vt, batch, seq)
    x1, h1 = mix(cv, y_a, sg, xf, w_pw2_b, b_pw2[l], w_o_b, norm_cross[l], tm=512)

    kvm = mem_kv(mem.reshape(batch * mem_len, d), norm_mem[l], w_kvm_b, tn=1024)
    x2, h2, (w_gate_b, w_up_b, w_down_b) = cross_attention(
        h1, x1, kvm, w_qm_b, w_om_b, norm_ffn[l], (w_gate[l], w_up[l], w_down[l]),
        batch, seq, mem_len, tm=256)

    a = ffn_up(h2, w_gate_b, w_up_b, tm=1024)
    out = ffn_down(a, w_down_b, x2, norm_final, tm=256)
    return out.reshape(batch, seq, d)
```
